```python
import jax, jax.numpy as jnp
from jax import lax
import numpy as np

D_MODEL = 1024
BATCH = 16
SEQ = 256
DEPTH = 2
DEC_BATCH = 2
DEC_SEQ = 2048
PAST_LEN = 512

GRID_W = 64
HEAD_DIM = 64
MIX_WIDTH = D_MODEL
NA_HEADS = MIX_WIDTH // (4 * HEAD_DIM)
GQA_Q_HEADS = MIX_WIDTH // (2 * HEAD_DIM)
GQA_KV_HEADS = GQA_Q_HEADS // 4
GQA_GROUP = GQA_Q_HEADS // GQA_KV_HEADS
FNET_CH = 64
FNET_GROUPS = MIX_WIDTH // (4 * FNET_CH)
NA_WIDTH = NA_HEADS * HEAD_DIM
GQA_WIDTH = GQA_Q_HEADS * HEAD_DIM
KV_WIDTH = GQA_KV_HEADS * HEAD_DIM
FNET_WIDTH = FNET_GROUPS * FNET_CH
IN_SPLITS = (NA_WIDTH, 2 * NA_WIDTH, 3 * NA_WIDTH, 3 * NA_WIDTH + GQA_WIDTH,
             3 * NA_WIDTH + GQA_WIDTH + KV_WIDTH, 3 * NA_WIDTH + GQA_WIDTH + 2 * KV_WIDTH)
IN_WIDTH = IN_SPLITS[-1] + FNET_WIDTH
NA_WIN_H = 8
NA_WIN_W = 16
Q_BLOCK = 128
ROPE_THETA = 10000.0
PEER_HEADS = 8
PEER_NKEYS = 128
PEER_EXPERTS = PEER_NKEYS * PEER_NKEYS
PEER_TOPK = 16
PEER_QDIM = 256
PEER_QHALF = PEER_QDIM // 2
PEER_TOK_BLOCK = 128
ALPHA = (2 * DEPTH) ** 0.25
BETA = (8 * DEPTH) ** -0.25
LN_EPS = 1e-5
RMS_EPS = 1e-6

kernel_name = "hybrid_natten_gqa_fnet_peer_diffusion_step"


def layer_norm(x, g, b):
    xf = x.astype(jnp.float32)
    mu = jnp.mean(xf, axis=-1, keepdims=True)
    var = jnp.mean(jnp.square(xf - mu), axis=-1, keepdims=True)
    return ((xf - mu) * lax.rsqrt(var + LN_EPS) * g + b).astype(x.dtype)


def rms_norm(x, g):
    xf = x.astype(jnp.float32)
    return (xf * lax.rsqrt(jnp.mean(jnp.square(xf), axis=-1, keepdims=True) + RMS_EPS) * g).astype(x.dtype)


def modulation(cond, w_mod, b_mod):
    return jnp.split(jax.nn.silu(cond) @ w_mod + b_mod, 6, axis=-1)


def axial_rope(x, n_tok):
    pos = jnp.arange(n_tok)
    rows = (pos // GRID_W).astype(jnp.float32)
    cols = (pos % GRID_W).astype(jnp.float32)
    half = HEAD_DIM // 2
    quarter = half // 2
    inv = 1.0 / (ROPE_THETA ** (jnp.arange(quarter, dtype=jnp.float32) / quarter))
    shape = (1, n_tok) + (1,) * (x.ndim - 3) + (quarter,)

    def rot(xp, ang):
        cos = jnp.cos(ang).reshape(shape).astype(x.dtype)
        sin = jnp.sin(ang).reshape(shape).astype(x.dtype)
        x1, x2 = xp[..., :quarter], xp[..., quarter:]
        return jnp.concatenate([x1 * cos - x2 * sin, x2 * cos + x1 * sin], axis=-1)

    return jnp.concatenate([rot(x[..., :half], rows[:, None] * inv),
                            rot(x[..., half:], cols[:, None] * inv)], axis=-1)


def blocked_attention(q, k, v):
    b, tq = q.shape[:2]
    nb = tq // Q_BLOCK
    qb = q.reshape((b, nb, Q_BLOCK) + q.shape[2:]).swapaxes(0, 1)
    scale = HEAD_DIM ** -0.5

    def one(qi):
        s = jnp.einsum('bqkgd,bskd->bkgqs', qi, k).astype(jnp.float32) * scale
        p = jax.nn.softmax(s, axis=-1).astype(v.dtype)
        return jnp.einsum('bkgqs,bskd->bqkgd', p, v)

    o = lax.map(one, qb)
    return o.swapaxes(0, 1).reshape(q.shape)


def neighbourhood_attention(qa, ka, va, ctx_k, ctx_v, rpb):
    b, t, h, dh = qa.shape
    rows = t // GRID_W
    kh = min(NA_WIN_H, rows)
    q5 = qa.reshape(b, rows, GRID_W, h, dh)
    k5 = ka.reshape(b, rows, GRID_W, h, dh)
    v5 = va.reshape(b, rows, GRID_W, h, dh)
    row_start = jnp.clip(jnp.arange(rows) - kh // 2, 0, rows - kh)
    col_start = jnp.clip(jnp.arange(GRID_W) - NA_WIN_W // 2, 0, GRID_W - NA_WIN_W)
    col_idx = col_start[:, None] + jnp.arange(NA_WIN_W)
    col_off = col_idx - jnp.arange(GRID_W)[:, None] + (NA_WIN_W - 1)
    rpb_cols = rpb[:, :, col_off]
    scale = HEAD_DIM ** -0.5
    n_win = kh * NA_WIN_W

    def one_row(args):
        r, q_r = args
        rs = row_start[r]
        k_win = lax.dynamic_slice_in_dim(k5, rs, kh, axis=1)[:, :, col_idx]
        v_win = lax.dynamic_slice_in_dim(v5, rs, kh, axis=1)[:, :, col_idx]
        row_off = rs + jnp.arange(kh) - r + (NA_WIN_H - 1)
        bias = rpb_cols[:, row_off].transpose(2, 0, 1, 3)
        s_win = jnp.einsum('bwhd,bjwkhd->bwhjk', q_r, k_win).astype(jnp.float32) * scale + bias[None]
        s_ctx = jnp.einsum('bwhd,blhd->bwhl', q_r, ctx_k).astype(jnp.float32) * scale
        s = jnp.concatenate([s_win.reshape(b, GRID_W, h, n_win), s_ctx], axis=-1)
        p = jax.nn.softmax(s, axis=-1).astype(q_r.dtype)
        p_win = p[..., :n_win].reshape(b, GRID_W, h, kh, NA_WIN_W)
        p_ctx = p[..., n_win:]
        return (jnp.einsum('bwhjk,bjwkhd->bwhd', p_win, v_win)
                + jnp.einsum('bwhl,blhd->bwhd', p_ctx, ctx_v))

    o = lax.map(one_row, (jnp.arange(rows), q5.swapaxes(0, 1)))
    return o.swapaxes(0, 1).reshape(b, t, h * dh)


def fourier_mix(xc, w_f):
    b, t, _ = xc.shape
    xg = xc.reshape(b, t, FNET_GROUPS, FNET_CH).astype(jnp.float32)
    f = jnp.fft.fft2(xg, axes=(1, 3), norm='ortho').real.astype(xc.dtype)
    return jnp.einsum('btgc,gce->btge', f, w_f).reshape(b, t, FNET_WIDTH)


def project(h, w_in, q_gain, k_gain):
    b, t, _ = h.shape
    qa, ka, va, qb, kb, vb, xc = jnp.split(h @ w_in, IN_SPLITS, axis=-1)
    heads = lambda z, n: z.reshape(b, t, n, HEAD_DIM)
    qb = rms_norm(qb.reshape(b, t, GQA_KV_HEADS, GQA_GROUP, HEAD_DIM), q_gain)
    kb = rms_norm(heads(kb, GQA_KV_HEADS), k_gain)
    return (heads(qa, NA_HEADS), heads(ka, NA_HEADS), heads(va, NA_HEADS),
            qb, kb, heads(vb, GQA_KV_HEADS), xc)


def token_mix_context(h, w_in, q_gain, k_gain, w_f, w_out):
    b, t, _ = h.shape
    qa, ka, va, qb, kb, vb, xc = project(h, w_in, q_gain, k_gain)
    oa = blocked_attention(qa[:, :, :, None], ka, va).reshape(b, t, NA_WIDTH)
    ob = blocked_attention(qb, kb, vb).reshape(b, t, GQA_WIDTH)
    oc = fourier_mix(xc, w_f)
    out = jnp.concatenate([oa, ob, oc], axis=-1) @ w_out
    return out, (ka, va, kb, vb)


def token_mix_latent(h, ctx_na_k, ctx_na_v, ctx_gqa_k, ctx_gqa_v, w_in, q_gain, k_gain, rpb, w_f, w_out):
    b, t, _ = h.shape
    qa, ka, va, qb, kb, vb, xc = project(h, w_in, q_gain, k_gain)
    oa = neighbourhood_attention(qa, ka, va, ctx_na_k, ctx_na_v, rpb)
    qb = axial_rope(qb, t)
    kb = axial_rope(kb, t)
    ob = blocked_attention(qb, jnp.concatenate([kb, ctx_gqa_k], axis=1),
                           jnp.concatenate([vb, ctx_gqa_v], axis=1)).reshape(b, t, GQA_WIDTH)
    oc = fourier_mix(xc, w_f)
    return jnp.concatenate([oa, ob, oc], axis=-1) @ w_out


def peer_ffn(h, w_q, sub_keys, u_tab, v_tab):
    b, t, d = h.shape
    xt = h.reshape(-1, d)
    n = xt.shape[0]
    q = (xt @ w_q).reshape(n, PEER_HEADS, 2, PEER_QHALF)
    s = jnp.einsum('nhpc,hpkc->nhpk', q, sub_keys).astype(jnp.float32)
    top_s, top_i = lax.top_k(s, PEER_TOPK)
    cand_s = top_s[:, :, 0, :, None] + top_s[:, :, 1, None, :]
    cand_i = top_i[:, :, 0, :, None] * PEER_NKEYS + top_i[:, :, 1, None, :]
    best_s, best_j = lax.top_k(cand_s.reshape(n, PEER_HEADS, -1), PEER_TOPK)
    experts = jnp.take_along_axis(cand_i.reshape(n, PEER_HEADS, -1), best_j, axis=-1)
    gates = jax.nn.softmax(best_s, axis=-1)
    nb = n // PEER_TOK_BLOCK
    kk = PEER_HEADS * PEER_TOPK
    xb = xt.reshape(nb, PEER_TOK_BLOCK, d)
    eb = experts.reshape(nb, PEER_TOK_BLOCK, kk)
    gb = gates.reshape(nb, PEER_TOK_BLOCK, kk)

    def one(args):
        x_blk, e_blk, g_blk = args
        a = jnp.einsum('tkd,td->tk', u_tab[e_blk], x_blk).astype(jnp.float32)
        w = (g_blk * jax.nn.gelu(a, approximate=False)).astype(x_blk.dtype)
        return jnp.einsum('tk,tkd->td', w, v_tab[e_blk])

    return lax.map(one, (xb, eb, gb)).reshape(b, t, d)


def setup_inputs(seed: int = 0) -> dict:
    key = jax.random.key(seed)
    ks = jax.random.split(key, 28)
    f32 = jnp.float32
    nrm = lambda k, shape, s: jax.random.normal(k, shape, f32) * s
    L, D = DEPTH, D_MODEL
    return {
        "x_prompt": nrm(ks[0], (BATCH, SEQ, D), 1.0),
        "x_sample": nrm(ks[1], (DEC_BATCH, DEC_SEQ, D), 1.0),
        "c": nrm(ks[2], (DEC_BATCH, D), 1.0),
        "cache_na_k": nrm(ks[3], (DEC_BATCH, L, PAST_LEN, NA_HEADS, HEAD_DIM), 1.0),
        "cache_na_v": nrm(ks[4], (DEC_BATCH, L, PAST_LEN, NA_HEADS, HEAD_DIM), 1.0),
        "cache_gqa_k": nrm(ks[5], (DEC_BATCH, L, PAST_LEN, GQA_KV_HEADS, HEAD_DIM), 1.0),
        "cache_gqa_v": nrm(ks[6], (DEC_BATCH, L, PAST_LEN, GQA_KV_HEADS, HEAD_DIM), 1.0),
        "c_ctx": nrm(ks[7], (D,), 1.0),
        "w_in": nrm(ks[8], (L, D, IN_WIDTH), D ** -0.5),
        "q_gain": 1.0 + nrm(ks[9], (L, HEAD_DIM), 0.02),
        "k_gain": 1.0 + nrm(ks[10], (L, HEAD_DIM), 0.02),
        "na_rpb": nrm(ks[11], (L, NA_HEADS, 2 * NA_WIN_H - 1, 2 * NA_WIN_W - 1), 0.1),
        "w_fourier": nrm(ks[12], (L, FNET_GROUPS, FNET_CH, FNET_CH), FNET_CH ** -0.5),
        "w_out": nrm(ks[13], (L, MIX_WIDTH, D), BETA * MIX_WIDTH ** -0.5),
        "w_mod": nrm(ks[14], (L, D, 6 * D), D ** -0.5),
        "b_mod": nrm(ks[15], (L, 6 * D), 0.02),
        "ln1_g": 1.0 + nrm(ks[16], (L, D), 0.02),
        "ln1_b": nrm(ks[17], (L, D), 0.02),
        "ln2_g": 1.0 + nrm(ks[18], (L, D), 0.02),
        "ln2_b": nrm(ks[19], (L, D), 0.02),
        "peer_wq": nrm(ks[20], (L, D, PEER_HEADS * PEER_QDIM), D ** -0.5),
        "peer_subkeys": nrm(ks[21], (L, PEER_HEADS, 2, PEER_NKEYS, PEER_QHALF), PEER_QHALF ** -0.5),
        "peer_u": nrm(ks[22], (L, PEER_EXPERTS, D), D ** -0.5),
        "peer_v": nrm(ks[23], (L, PEER_EXPERTS, D), BETA),
    }


def reference(x_prompt, x_sample, c, cache_na_k, cache_na_v, cache_gqa_k, cache_gqa_v, c_ctx,
              w_in, q_gain, k_gain, na_rpb, w_fourier, w_out, w_mod, b_mod,
              ln1_g, ln1_b, ln2_g, ln2_b, peer_wq, peer_subkeys, peer_u, peer_v):
    yp, ys = x_prompt, x_sample
    na_k, na_v, gqa_k, gqa_v = [], [], [], []
    for l in range(DEPTH):
        proj = (w_in[l], q_gain[l], k_gain[l])
        ffn = (peer_wq[l], peer_subkeys[l], peer_u[l], peer_v[l])
        sh1, sc1, g1, sh2, sc2, g2 = modulation(c_ctx, w_mod[l], b_mod[l])
        mix, (ka, va, kb, vb) = token_mix_context(yp * (1 + sc1) + sh1, *proj, w_fourier[l], w_out[l])
        yp = layer_norm(ALPHA * yp + g1 * mix, ln1_g[l], ln1_b[l])
        yp = layer_norm(ALPHA * yp + g2 * peer_ffn(yp * (1 + sc2) + sh2, *ffn), ln2_g[l], ln2_b[l])
        na_k.append(ka)
        na_v.append(va)
        gqa_k.append(kb)
        gqa_v.append(vb)
        sh1, sc1, g1, sh2, sc2, g2 = modulation(c[:, None, :], w_mod[l], b_mod[l])
        mix = token_mix_latent(ys * (1 + sc1) + sh1, cache_na_k[:, l], cache_na_v[:, l],
                               cache_gqa_k[:, l], cache_gqa_v[:, l], *proj, na_rpb[l],
                               w_fourier[l], w_out[l])
        ys = layer_norm(ALPHA * ys + g1 * mix, ln1_g[l], ln1_b[l])
        ys = layer_norm(ALPHA * ys + g2 * peer_ffn(ys * (1 + sc2) + sh2, *ffn), ln2_g[l], ln2_b[l])
    return (yp, ys, jnp.stack(na_k, axis=1), jnp.stack(na_v, axis=1),
            jnp.stack(gqa_k, axis=1), jnp.stack(gqa_v, axis=1))
```

```python
import functools
import math

import numpy as np
import jax
import jax.numpy as jnp
from jax import lax
from jax.experimental import pallas as pl
from jax.experimental.pallas import tpu as pltpu

f32 = jnp.float32
bf16 = jnp.bfloat16

D_MODEL = 1024
DEPTH = 2
GRID_W = 64
HEAD_DIM = 64
NA_HEADS = 4
GQA_Q_HEADS = 8
GQA_KV_HEADS = 2
GQA_GROUP = 4
FNET_CH = 64
FNET_GROUPS = 4
NA_WIDTH = 256
GQA_WIDTH = 512
KV_WIDTH = 128
FNET_WIDTH = 256
IN_WIDTH = 1792
OFF_QA, OFF_KA, OFF_VA, OFF_QB, OFF_KB, OFF_VB, OFF_XC = 0, 256, 512, 768, 1280, 1408, 1536
NA_WIN_H = 8
NA_WIN_W = 16
ROPE_THETA = 10000.0
PEER_HEADS = 8
PEER_NKEYS = 128
PEER_EXPERTS = PEER_NKEYS * PEER_NKEYS
PEER_TOPK = 16
PEER_QDIM = 256
ALPHA = (2 * DEPTH) ** 0.25
LN_EPS = 1e-5
RMS_EPS = 1e-6
ATTN_SCALE = HEAD_DIM ** -0.5
INV_SQRT2 = 1.0 / math.sqrt(2.0)
NEG_BIG = -1e30

LANES = 128
SUBLANES = 8
V7X_VMEM_BYTES = 64 * 2**20
V7X_VMEM_LIMIT_CAP = 56 * 2**20

ROW_TILE = 512
PEER_TOK = 512
PEER_ECHUNK = 1024
PEER_IB = PEER_ECHUNK // PEER_NKEYS
GQA_QBLK = 256
DFT_ROWS = 256
SCORE_PITCH = 136

NT_DIMS = (((1,), (1,)), ((), ()))


def _params(sem, est_bytes):
    limit = int(min(V7X_VMEM_LIMIT_CAP, max(16 * 2**20, est_bytes * 5 // 4 + 4 * 2**20)))
    return pltpu.CompilerParams(dimension_semantics=sem, vmem_limit_bytes=limit)


def _layer_norm(z, g, b):
    mu = jnp.mean(z, axis=-1, keepdims=True)
    zc = z - mu
    var = jnp.mean(zc * zc, axis=-1, keepdims=True)
    return zc * lax.rsqrt(var + LN_EPS) * g + b


def _mod_kernel(cond_ref, w_ref, b_ref, o_ref):
    c = cond_ref[...]
    a = (c / (1.0 + jnp.exp(-c))).astype(bf16)
    o_ref[...] = jnp.dot(a, w_ref[...].astype(bf16), preferred_element_type=f32) + b_ref[...]


def _modulation(cond, w_mod, b_mod):
    tn = 1536
    nl = w_mod.shape[0]
    return pl.pallas_call(
        _mod_kernel,
        out_shape=jax.ShapeDtypeStruct((nl, SUBLANES, 6 * D_MODEL), f32),
        grid=(nl, 6 * D_MODEL // tn),
        in_specs=[
            pl.BlockSpec((SUBLANES, D_MODEL), lambda l, j: (0, 0)),
            pl.BlockSpec((None, D_MODEL, tn), lambda l, j: (l, 0, j)),
            pl.BlockSpec((None, 1, tn), lambda l, j: (l, 0, j)),
        ],
        out_specs=pl.BlockSpec((None, SUBLANES, tn), lambda l, j: (l, 0, j)),
        compiler_params=_params(("parallel", "parallel"), 2 * D_MODEL * tn * 4 + D_MODEL * tn * 2),
        name="modulation",
    )(cond, w_mod, b_mod.reshape(nl, 1, 6 * D_MODEL))


def _inproj_kernel(*refs, rope):
    if rope:
        (x_ref, mod_ref, w_ref, bd_ref, gq_ref, gk_ref, gqs_ref, gks_ref, cos_ref, sin_ref,
         qa_ref, ka_ref, va_ref, qb_ref, kb_ref, vb_ref, xc_ref) = refs
    else:
        (x_ref, mod_ref, w_ref, bd_ref, gq_ref, gk_ref,
         qa_ref, ka_ref, va_ref, qb_ref, kb_ref, vb_ref, xc_ref) = refs
    h = (x_ref[...] * (1.0 + mod_ref[1:2, :]) + mod_ref[0:1, :]).astype(bf16)
    p = jnp.dot(h, w_ref[...], preferred_element_type=f32)
    qa_ref[...] = p[:, OFF_QA:OFF_KA]
    ka_ref[...] = p[:, OFF_KA:OFF_VA]
    va_ref[...] = p[:, OFF_VA:OFF_QB]
    vb_ref[...] = p[:, OFF_VB:OFF_XC]
    xc_ref[...] = p[:, OFF_XC:IN_WIDTH].astype(bf16)
    bd = bd_ref[...]

    def normed(off, off_sw, g_ref, gs_ref, c):
        z = p[:, off + c * LANES:off + (c + 1) * LANES]
        r = lax.rsqrt(jnp.dot(z * z, bd, precision=lax.Precision.HIGHEST, preferred_element_type=f32) + RMS_EPS)
        zn = z * r * g_ref[:, c * LANES:(c + 1) * LANES]
        if rope:
            zs = p[:, off_sw + c * LANES:off_sw + (c + 1) * LANES] * r * gs_ref[:, c * LANES:(c + 1) * LANES]
            zn = zn * cos_ref[...] + zs * sin_ref[...]
        return zn

    for c in range(GQA_WIDTH // LANES):
        qb_ref[:, c * LANES:(c + 1) * LANES] = normed(OFF_QB, IN_WIDTH, gq_ref, gqs_ref if rope else None, c)
    kb_ref[...] = normed(OFF_KB, IN_WIDTH + GQA_WIDTH, gk_ref, gks_ref if rope else None, 0)


def _inproj(x, mod, w_ext, bd, gq, gk, rope_args, tiles_per_batch):
    n = x.shape[0]
    rope = rope_args is not None
    w_cols = w_ext.shape[1]
    const = lambda shape: pl.BlockSpec(shape, lambda i: (0,) * len(shape))
    in_specs = [
        pl.BlockSpec((ROW_TILE, D_MODEL), lambda i: (i, 0)),
        pl.BlockSpec((None, 6, D_MODEL), lambda i: (i // tiles_per_batch, 0, 0)),
        const((D_MODEL, w_cols)),
        const((LANES, LANES)),
        const((1, GQA_WIDTH)),
        const((1, KV_WIDTH)),
    ]
    args = [x, mod, w_ext, bd, gq, gk]
    if rope:
        gqs, gks, cos_t, sin_t = rope_args
        in_specs += [
            const((1, GQA_WIDTH)),
            const((1, KV_WIDTH)),
            pl.BlockSpec((ROW_TILE, LANES), lambda i: (i % tiles_per_batch, 0)),
            pl.BlockSpec((ROW_TILE, LANES), lambda i: (i % tiles_per_batch, 0)),
        ]
        args += [gqs, gks, cos_t, sin_t]
    widths = [(NA_WIDTH, f32), (NA_WIDTH, f32), (NA_WIDTH, f32), (GQA_WIDTH, f32), (KV_WIDTH, f32),
              (KV_WIDTH, f32), (FNET_WIDTH, bf16)]
    est = 2 * ROW_TILE * D_MODEL * 4 + 2 * D_MODEL * w_cols * 2 + 3 * ROW_TILE * w_cols * 4
    return pl.pallas_call(
        functools.partial(_inproj_kernel, rope=rope),
        out_shape=[jax.ShapeDtypeStruct((n, w), dt) for w, dt in widths],
        grid=(n // ROW_TILE,),
        in_specs=in_specs,
        out_specs=[pl.BlockSpec((ROW_TILE, w), lambda i: (i, 0)) for w, _ in widths],
        compiler_params=_params(("parallel",), est),
        name="inproj_rope" if rope else "inproj",
    )(*args)


def _scores(q, k):
    return lax.dot_general((q * ATTN_SCALE).astype(bf16), k.astype(bf16), NT_DIMS, preferred_element_type=f32)


def _softmax_pv(s_list, v_list):
    m = s_list[0].max(axis=-1, keepdims=True)
    for s in s_list[1:]:
        m = jnp.maximum(m, s.max(axis=-1, keepdims=True))
    den = None
    out = None
    for s, v in zip(s_list, v_list):
        e = jnp.exp(s - m)
        d = e.sum(axis=-1, keepdims=True)
        o = jnp.dot(e.astype(bf16), v.astype(bf16), preferred_element_type=f32)
        den = d if den is None else den + d
        out = o if out is None else out + o
    return out / den


def _ctx_attn_kernel(qa_ref, ka_ref, va_ref, qb_ref, kb_ref, vb_ref, oa_ref, ob_ref):
    for h in range(NA_HEADS):
        sl = slice(h * HEAD_DIM, (h + 1) * HEAD_DIM)
        oa_ref[:, sl] = _softmax_pv([_scores(qa_ref[:, sl], ka_ref[:, sl])], [va_ref[:, sl]])
    for kv in range(GQA_KV_HEADS):
        ksl = slice(kv * HEAD_DIM, (kv + 1) * HEAD_DIM)
        k = kb_ref[:, ksl]
        v = vb_ref[:, ksl]
        for g in range(GQA_GROUP):
            hd = kv * GQA_GROUP + g
            sl = slice(hd * HEAD_DIM, (hd + 1) * HEAD_DIM)
            ob_ref[:, sl] = _softmax_pv([_scores(qb_ref[:, sl], k)], [v])


def _ctx_attention(qa, ka, va, qb, kb, vb, t):
    n = qa.shape[0]
    spec = lambda w: pl.BlockSpec((t, w), lambda b: (b, 0))
    return pl.pallas_call(
        _ctx_attn_kernel,
        out_shape=[jax.ShapeDtypeStruct((n, NA_WIDTH), f32), jax.ShapeDtypeStruct((n, GQA_WIDTH), f32)],
        grid=(n // t,),
        in_specs=[spec(NA_WIDTH), spec(NA_WIDTH), spec(NA_WIDTH), spec(GQA_WIDTH), spec(KV_WIDTH), spec(KV_WIDTH)],
        out_specs=[spec(NA_WIDTH), spec(GQA_WIDTH)],
        compiler_params=_params(("parallel",), 16 * t * D_MODEL),
        name="ctx_attention",
    )(qa, ka, va, qb, kb, vb)


def _lat_na_kernel(qa_ref, ka_ref, va_ref, ck_ref, cv_ref, bias_ref, oa_ref, *, rows):
    r = pl.program_id(1)
    rs = jnp.clip(r - NA_WIN_H // 2, 0, rows - NA_WIN_H)
    variant = rs - r + (NA_WIN_H - 1)
    start = pl.multiple_of(rs * GRID_W, GRID_W)
    kwin = ka_ref[pl.ds(start, NA_WIN_H * GRID_W), :]
    vwin = va_ref[pl.ds(start, NA_WIN_H * GRID_W), :]
    for h in range(NA_HEADS):
        sl = slice(h * HEAD_DIM, (h + 1) * HEAD_DIM)
        q = qa_ref[:, sl]
        s_win = _scores(q, kwin[:, sl]) + bias_ref[h, variant]
        s_ctx = _scores(q, ck_ref[:, sl])
        oa_ref[:, sl] = _softmax_pv([s_win, s_ctx], [vwin[:, sl], cv_ref[:, sl]])


def _lat_na_attention(qa, ka, va, ck, cv, bias, nb, t):
    rows = t // GRID_W
    past = ck.shape[1]
    n = qa.shape[0]
    return pl.pallas_call(
        functools.partial(_lat_na_kernel, rows=rows),
        out_shape=jax.ShapeDtypeStruct((n, NA_WIDTH), f32),
        grid=(nb, rows),
        in_specs=[
            pl.BlockSpec((GRID_W, NA_WIDTH), lambda b, r: (b * rows + r, 0)),
            pl.BlockSpec((t, NA_WIDTH), lambda b, r: (b, 0)),
            pl.BlockSpec((t, NA_WIDTH), lambda b, r: (b, 0)),
            pl.BlockSpec((None, past, NA_WIDTH), lambda b, r: (b, 0, 0)),
            pl.BlockSpec((None, past, NA_WIDTH), lambda b, r: (b, 0, 0)),
            pl.BlockSpec(bias.shape, lambda b, r: (0, 0, 0, 0)),
        ],
        out_specs=pl.BlockSpec((GRID_W, NA_WIDTH), lambda b, r: (b * rows + r, 0)),
        compiler_params=_params(("parallel", "parallel"), 2 * bias.size * 4 + 8 * t * NA_WIDTH * 4),
        name="latent_na_attention",
    )(qa, ka, va, ck, cv, bias)


def _lat_gqa_kernel(qb_ref, kb_ref, vb_ref, ck_ref, cv_ref, ob_ref):
    for kv in range(GQA_KV_HEADS):
        ksl = slice(kv * HEAD_DIM, (kv + 1) * HEAD_DIM)
        k, v, ck, cv = kb_ref[:, ksl], vb_ref[:, ksl], ck_ref[:, ksl], cv_ref[:, ksl]
        for g in range(GQA_GROUP):
            hd = kv * GQA_GROUP + g
            sl = slice(hd * HEAD_DIM, (hd + 1) * HEAD_DIM)
            q = qb_ref[:, sl]
            ob_ref[:, sl] = _softmax_pv([_scores(q, k), _scores(q, ck)], [v, cv])


def _lat_gqa_attention(qb, kb, vb, ck, cv, nb, t):
    n = qb.shape[0]
    past = ck.shape[1]
    nq = t // GQA_QBLK
    return pl.pallas_call(
        _lat_gqa_kernel,
        out_shape=jax.ShapeDtypeStruct((n, GQA_WIDTH), f32),
        grid=(nb, nq),
        in_specs=[
            pl.BlockSpec((GQA_QBLK, GQA_WIDTH), lambda b, i: (b * nq + i, 0)),
            pl.BlockSpec((t, KV_WIDTH), lambda b, i: (b, 0)),
            pl.BlockSpec((t, KV_WIDTH), lambda b, i: (b, 0)),
            pl.BlockSpec((None, past, KV_WIDTH), lambda b, i: (b, 0, 0)),
            pl.BlockSpec((None, past, KV_WIDTH), lambda b, i: (b, 0, 0)),
        ],
        out_specs=pl.BlockSpec((GQA_QBLK, GQA_WIDTH), lambda b, i: (b * nq + i, 0)),
        compiler_params=_params(("parallel", "parallel"), 6 * GQA_QBLK * (t + past) * 4 + 8 * t * KV_WIDTH * 4),
        name="latent_gqa_attention",
    )(qb, kb, vb, ck, cv)


def _fourier_kernel(ct_ref, st_ref, x_ref, cc_ref, sc_ref, wf_ref, o_ref):
    x = x_ref[...]
    yc = jnp.dot(ct_ref[...], x, preferred_element_type=f32).astype(bf16)
    ys = jnp.dot(st_ref[...], x, preferred_element_type=f32).astype(bf16)
    z = (jnp.dot(yc, cc_ref[...], preferred_element_type=f32)
         - jnp.dot(ys, sc_ref[...], preferred_element_type=f32))
    o_ref[...] = jnp.dot(z.astype(bf16), wf_ref[...], preferred_element_type=f32)


def _dft_tables(t):
    k = np.arange(t, dtype=np.int64)
    ang = 2.0 * np.pi * ((k[:, None] * k[None, :]) % t).astype(np.float64) / t
    return jnp.asarray(np.cos(ang), dtype=f32).astype(bf16), jnp.asarray(np.sin(ang), dtype=f32).astype(bf16)


def _channel_tables(t):
    c = np.arange(FNET_CH, dtype=np.int64)
    ang = 2.0 * np.pi * ((c[:, None] * c[None, :]) % FNET_CH).astype(np.float64) / FNET_CH
    norm = 1.0 / math.sqrt(t * FNET_CH)
    eye = np.eye(FNET_GROUPS)
    return (jnp.asarray(np.kron(eye, np.cos(ang) * norm), dtype=f32).astype(bf16),
            jnp.asarray(np.kron(eye, np.sin(ang) * norm), dtype=f32).astype(bf16))


def _fourier(xc, wf_bd, nb, t):
    n = xc.shape[0]
    rows = min(t, DFT_ROWS)
    nt = t // rows
    ct, st = _dft_tables(t)
    cc, sc = _channel_tables(t)
    const = lambda: pl.BlockSpec((FNET_WIDTH, FNET_WIDTH), lambda b, i: (0, 0))
    return pl.pallas_call(
        _fourier_kernel,
        out_shape=jax.ShapeDtypeStruct((n, FNET_WIDTH), f32),
        grid=(nb, nt),
        in_specs=[
            pl.BlockSpec((rows, t), lambda b, i: (i, 0)),
            pl.BlockSpec((rows, t), lambda b, i: (i, 0)),
            pl.BlockSpec((t, FNET_WIDTH), lambda b, i: (b, 0)),
            const(), const(), const(),
        ],
        out_specs=pl.BlockSpec((rows, FNET_WIDTH), lambda b, i: (b * nt + i, 0)),
        compiler_params=_params(("parallel", "parallel"), 8 * rows * t + 4 * t * FNET_WIDTH),
        name="fourier_mix",
    )(ct, st, xc, cc, sc, wf_bd)


def _outproj_kernel(oa_ref, ob_ref, oc_ref, w_ref, x_ref, mod_ref, g_ref, b_ref, y_ref, h_ref):
    o1, o2 = NA_WIDTH, NA_WIDTH + GQA_WIDTH
    mix = (jnp.dot(oa_ref[...].astype(bf16), w_ref[0:o1, :], preferred_element_type=f32)
           + jnp.dot(ob_ref[...].astype(bf16), w_ref[o1:o2, :], preferred_element_type=f32)
           + jnp.dot(oc_ref[...].astype(bf16), w_ref[o2:D_MODEL, :], preferred_element_type=f32))
    y = _layer_norm(ALPHA * x_ref[...] + mod_ref[2:3, :] * mix, g_ref[...], b_ref[...])
    y_ref[...] = y
    h_ref[...] = (y * (1.0 + mod_ref[4:5, :]) + mod_ref[3:4, :]).astype(bf16)


def _outproj(oa, ob, oc, w_out, x, mod, g, b, tiles_per_batch):
    n = x.shape[0]
    row = lambda w: pl.BlockSpec((ROW_TILE, w), lambda i: (i, 0))
    vec = lambda: pl.BlockSpec((1, D_MODEL), lambda i: (0, 0))
    return pl.pallas_call(
        _outproj_kernel,
        out_shape=[jax.ShapeDtypeStruct((n, D_MODEL), f32), jax.ShapeDtypeStruct((n, D_MODEL), bf16)],
        grid=(n // ROW_TILE,),
        in_specs=[
            row(NA_WIDTH), row(GQA_WIDTH), row(FNET_WIDTH),
            pl.BlockSpec((D_MODEL, D_MODEL), lambda i: (0, 0)),
            row(D_MODEL),
            pl.BlockSpec((None, 6, D_MODEL), lambda i: (i // tiles_per_batch, 0, 0)),
            vec(), vec(),
        ],
        out_specs=[row(D_MODEL), row(D_MODEL)],
        compiler_params=_params(("parallel",), 10 * ROW_TILE * D_MODEL * 4 + 4 * D_MODEL * D_MODEL),
        name="outproj_ln",
    )(oa, ob, oc, w_out, x, mod, g, b)


def _mx(a, b):
    if a is None:
        return b
    if b is None:
        return a
    return jnp.maximum(a, b)


def _mn(a, b):
    if a is None or b is None:
        return None
    return jnp.minimum(a, b)


def _oddeven_pairs(n):
    pairs = []
    p = 1
    while p < n:
        k = p
        while k >= 1:
            for j in range(k % p, n - k, 2 * k):
                for i in range(min(k, n - j - k)):
                    if (i + j) // (2 * p) == (i + j + k) // (2 * p):
                        pairs.append((i + j, i + j + k))
            k //= 2
        p *= 2
    return pairs


_SORT16 = _oddeven_pairs(PEER_TOPK)


def _sort16(v):
    v = list(v)
    for i, j in _SORT16:
        v[i], v[j] = _mx(v[i], v[j]), _mn(v[i], v[j])
    return v


def _merge_top16(a, b):
    c = [_mx(a[k], b[PEER_TOPK - 1 - k]) for k in range(PEER_TOPK)]
    d = PEER_TOPK // 2
    while d >= 1:
        for i in range(PEER_TOPK):
            if i & d == 0:
                c[i], c[i + d] = _mx(c[i], c[i + d]), _mn(c[i], c[i + d])
        d //= 2
    return c


def _top16_sorted(vals):
    lists = [_sort16(vals[i:i + PEER_TOPK]) for i in range(0, len(vals), PEER_TOPK)]
    while len(lists) > 1:
        lists = [_merge_top16(lists[i], lists[i + 1]) for i in range(0, len(lists), 2)]
    return lists[0]


def _kth_pair_sum(a, b):
    cands = {}
    lists = []
    for j in range(PEER_TOPK):
        col = [a[i] + b[j] for i in range(PEER_TOPK // (j + 1))]
        for i, s in enumerate(col):
            cands[(i, j)] = s
        lists.append(col + [None] * (PEER_TOPK - len(col)))
    while len(lists) > 1:
        nxt = [_merge_top16(lists[i], lists[i + 1]) for i in range(0, len(lists) - 1, 2)]
        if len(lists) % 2:
            nxt.append(lists[-1])
        lists = nxt
    return lists[0][PEER_TOPK - 1], cands


def _peer_q_kernel(h_ref, wq_ref, keys_ref, s1_ref, s2_ref, tau_ref, m1_ref, m2_ref, rz_ref, scr):
    q = jnp.dot(h_ref[...], wq_ref[...], preferred_element_type=f32)
    ntile = PEER_TOK // LANES
    for g in range(2 * PEER_HEADS):
        hd, half = divmod(g, 2)
        qg = q[:, g * PEER_NKEYS:(g + 1) * PEER_NKEYS].astype(bf16)
        st = lax.dot_general(keys_ref[g].astype(bf16), qg, NT_DIMS, preferred_element_type=f32)
        if half == 0:
            s1_ref[hd] = st
        else:
            s2_ref[hd] = st
        base = (half * PEER_HEADS + hd) * SCORE_PITCH
        for tt in range(ntile):
            scr[tt, base:base + PEER_NKEYS, :] = st[:, tt * LANES:(tt + 1) * LANES]

    def tile(tt, carry):
        tops = []
        for half in range(2):
            vals = [scr[tt, pl.ds(half * PEER_HEADS * SCORE_PITCH + k, PEER_HEADS, stride=SCORE_PITCH), :]
                    for k in range(PEER_NKEYS)]
            tops.append(_top16_sorted(vals))
        a, b = tops
        tau, cands = _kth_pair_sum(a, b)
        ea = [jnp.exp(x - a[0]) for x in a]
        eb = [jnp.exp(x - b[0]) for x in b]
        z = None
        for (i, j), s in cands.items():
            term = jnp.where(s >= tau, ea[i] * eb[j], 0.0)
            z = term if z is None else z + term
        col = pl.ds(pl.multiple_of(tt * LANES, LANES), LANES)
        tau_ref[:, col] = tau
        m1_ref[:, col] = a[0]
        m2_ref[:, col] = b[0]
        rz_ref[:, col] = 1.0 / z
        return carry

    lax.fori_loop(0, ntile, tile, 0)


def _peer_query(h2, wq, keys):
    n = h2.shape[0]
    qw = PEER_HEADS * PEER_QDIM
    sshape = jax.ShapeDtypeStruct((PEER_HEADS, PEER_NKEYS, n), f32)
    vshape = jax.ShapeDtypeStruct((PEER_HEADS, n), f32)
    sspec = pl.BlockSpec((PEER_HEADS, PEER_NKEYS, PEER_TOK), lambda i: (0, 0, i))
    vspec = pl.BlockSpec((PEER_HEADS, PEER_TOK), lambda i: (0, i))
    scr_shape = (PEER_TOK // LANES, 2 * PEER_HEADS * SCORE_PITCH, LANES)
    est = (2 * PEER_TOK * D_MODEL * 2 + 2 * D_MODEL * qw * 2 + 2 * PEER_TOK * qw * 4
           + 4 * PEER_HEADS * PEER_NKEYS * PEER_TOK * 4 + int(np.prod(scr_shape)) * 4)
    return pl.pallas_call(
        _peer_q_kernel,
        out_shape=[sshape, sshape, vshape, vshape, vshape, vshape],
        grid=(n // PEER_TOK,),
        in_specs=[
            pl.BlockSpec((PEER_TOK, D_MODEL), lambda i: (i, 0)),
            pl.BlockSpec((D_MODEL, qw), lambda i: (0, 0)),
            pl.BlockSpec((2 * PEER_HEADS, PEER_NKEYS, PEER_QDIM // 2), lambda i: (0, 0, 0)),
        ],
        out_specs=[sspec, sspec, vspec, vspec, vspec, vspec],
        scratch_shapes=[pltpu.VMEM(scr_shape, f32)],
        compiler_params=_params(("parallel",), est),
        name="peer_query_topk",
    )(h2, wq, keys)


def _peer_kernel(h_ref, u_ref, vt_ref, s1_ref, s2_ref, tau_ref, m1_ref, m2_ref, rz_ref,
                 y_ref, mod_ref, g_ref, b_ref, o_ref, e2_scr, a_scr, w_scr, acc_scr):
    c = pl.program_id(1)
    ntile = PEER_TOK // LANES

    @pl.when(c == 0)
    def _():
        for hd in range(PEER_HEADS):
            e2_scr[hd] = jnp.exp(s2_ref[hd] - m2_ref[hd:hd + 1, :]) * rz_ref[hd:hd + 1, :]
        acc_scr[...] = jnp.zeros_like(acc_scr)

    a_scr[...] = lax.dot_general(u_ref[...], h_ref[...], NT_DIMS, preferred_element_type=f32)

    def tile(tt, carry):
        col = pl.ds(pl.multiple_of(tt * LANES, LANES), LANES)
        for ib in range(PEER_IB):
            row = slice(ib * PEER_NKEYS, (ib + 1) * PEER_NKEYS)
            gate = jnp.zeros((PEER_NKEYS, LANES), f32)
            for hd in range(PEER_HEADS):
                s1r = s1_ref[hd, ib:ib + 1, col]
                e1r = jnp.exp(s1r - m1_ref[hd:hd + 1, col])
                keep = (s1r + s2_ref[hd, :, col]) >= tau_ref[hd:hd + 1, col]
                gate = gate + jnp.where(keep, e1r * e2_scr[hd, :, col], 0.0)
            a = a_scr[row, col]
            w_scr[row, col] = (0.5 * a * (1.0 + lax.erf(a * INV_SQRT2)) * gate).astype(bf16)
        return carry

    lax.fori_loop(0, ntile, tile, 0)
    acc_scr[...] += jnp.dot(vt_ref[...], w_scr[...], preferred_element_type=f32)

    @pl.when(c == pl.num_programs(1) - 1)
    def _():
        peer = acc_scr[...].T
        o_ref[...] = _layer_norm(ALPHA * y_ref[...] + mod_ref[5:6, :] * peer, g_ref[...], b_ref[...])


def _peer(h2, u, vt, s1, s2, tau, m1, m2, rz, y1, mod, g, b, tiles_per_batch):
    n = h2.shape[0]
    sspec = pl.BlockSpec((PEER_HEADS, PEER_NKEYS, PEER_TOK), lambda t, c: (0, 0, t))
    vspec = pl.BlockSpec((PEER_HEADS, PEER_TOK), lambda t, c: (0, t))
    vec = lambda: pl.BlockSpec((1, D_MODEL), lambda t, c: (0, 0))
    sbytes = PEER_HEADS * PEER_NKEYS * PEER_TOK * 4
    est = (6 * sbytes + 4 * PEER_ECHUNK * D_MODEL * 2 + 2 * PEER_TOK * D_MODEL * (2 + 4 + 4)
           + PEER_ECHUNK * PEER_TOK * 6 + D_MODEL * PEER_TOK * 4)
    return pl.pallas_call(
        _peer_kernel,
        out_shape=jax.ShapeDtypeStruct((n, D_MODEL), f32),
        grid=(n // PEER_TOK, PEER_EXPERTS // PEER_ECHUNK),
        in_specs=[
            pl.BlockSpec((PEER_TOK, D_MODEL), lambda t, c: (t, 0)),
            pl.BlockSpec((PEER_ECHUNK, D_MODEL), lambda t, c: (c, 0)),
            pl.BlockSpec((D_MODEL, PEER_ECHUNK), lambda t, c: (0, c)),
            pl.BlockSpec((PEER_HEADS, PEER_IB, PEER_TOK), lambda t, c: (0, c, t)),
            sspec, vspec, vspec, vspec, vspec,
            pl.BlockSpec((PEER_TOK, D_MODEL), lambda t, c: (t, 0)),
            pl.BlockSpec((None, 6, D_MODEL), lambda t, c: (t // tiles_per_batch, 0, 0)),
            vec(), vec(),
        ],
        out_specs=pl.BlockSpec((PEER_TOK, D_MODEL), lambda t, c: (t, 0)),
        scratch_shapes=[
            pltpu.VMEM((PEER_HEADS, PEER_NKEYS, PEER_TOK), f32),
            pltpu.VMEM((PEER_ECHUNK, PEER_TOK), f32),
            pltpu.VMEM((PEER_ECHUNK, PEER_TOK), bf16),
            pltpu.VMEM((D_MODEL, PEER_TOK), f32),
        ],
        compiler_params=_params(("parallel", "arbitrary"), est),
        name="peer_experts",
    )(h2, u, vt, s1, s2, tau, m1, m2, rz, y1, mod, g, b)


_QUARTER = HEAD_DIM // 4
_SWAP64 = np.concatenate([np.arange(_QUARTER, 2 * _QUARTER), np.arange(0, _QUARTER),
                          np.arange(3 * _QUARTER, 4 * _QUARTER), np.arange(2 * _QUARTER, 3 * _QUARTER)])


def _rope_tables(t):
    pos = jnp.arange(t)
    rows = (pos // GRID_W).astype(f32)
    cols = (pos % GRID_W).astype(f32)
    inv = 1.0 / (ROPE_THETA ** (jnp.arange(_QUARTER, dtype=f32) / _QUARTER))
    ar, ac = rows[:, None] * inv, cols[:, None] * inv
    cos64 = jnp.concatenate([jnp.cos(ar), jnp.cos(ar), jnp.cos(ac), jnp.cos(ac)], axis=-1)
    sin64 = jnp.concatenate([-jnp.sin(ar), jnp.sin(ar), -jnp.sin(ac), jnp.sin(ac)], axis=-1)
    return jnp.tile(cos64, (1, 2)), jnp.tile(sin64, (1, 2))


def _head_mean_matrix():
    return jnp.asarray(np.kron(np.eye(LANES // HEAD_DIM), np.full((HEAD_DIM, HEAD_DIM), 1.0 / HEAD_DIM)), dtype=f32)


def _na_bias_table(rpb):
    w = np.arange(GRID_W)[:, None]
    c = np.arange(GRID_W)[None, :]
    cs = np.clip(w - NA_WIN_W // 2, 0, GRID_W - NA_WIN_W)
    inwin = (c >= cs) & (c < cs + NA_WIN_W)
    coff = np.clip(c - w + (NA_WIN_W - 1), 0, 2 * NA_WIN_W - 2)
    roff = np.arange(NA_WIN_H)[:, None] + np.arange(NA_WIN_H)[None, :]
    t = rpb[:, roff][:, :, :, coff]
    t = jnp.where(inwin[None, None, None], t, NEG_BIG)
    return t.transpose(0, 1, 3, 2, 4).reshape(NA_HEADS, NA_WIN_H, GRID_W, NA_WIN_H * GRID_W)


def _block_diag(w):
    g, c, _ = w.shape
    eye = jnp.eye(g, dtype=w.dtype)
    return (eye[:, None, :, None] * w[:, :, None, :]).reshape(g * c, g * c)


def _mod_rows(mod_l, start, count):
    return mod_l[start:start + count].reshape(count, 6, D_MODEL)


def kernel(x_prompt, x_sample, c, cache_na_k, cache_na_v, cache_gqa_k, cache_gqa_v, c_ctx, w_in, q_gain, k_gain,
           na_rpb, w_fourier, w_out, w_mod, b_mod, ln1_g, ln1_b, ln2_g, ln2_b, peer_wq, peer_subkeys, peer_u, peer_v):
    batch, seq, _ = x_prompt.shape
    dec_batch, dec_seq, _ = x_sample.shape
    past = cache_na_k.shape[2]
    n_ctx, n_lat = batch * seq, dec_batch * dec_seq

    cond = jnp.zeros((SUBLANES, D_MODEL), f32).at[0].set(c_ctx).at[1:1 + dec_batch].set(c)
    mod = _modulation(cond, w_mod, b_mod)

    bd = _head_mean_matrix()
    cos_t, sin_t = _rope_tables(dec_seq)
    swap_q = np.concatenate([h * HEAD_DIM + _SWAP64 for h in range(GQA_Q_HEADS)])
    swap_k = np.concatenate([h * HEAD_DIM + _SWAP64 for h in range(GQA_KV_HEADS)])

    yp = x_prompt.reshape(n_ctx, D_MODEL)
    ys = x_sample.reshape(n_lat, D_MODEL)
    na_k, na_v, gqa_k, gqa_v = [], [], [], []
    for l in range(DEPTH):
        w_in_l = w_in[l]
        w_ctx = w_in_l.astype(bf16)
        w_lat = jnp.concatenate([w_in_l, w_in_l[:, OFF_QB + swap_q], w_in_l[:, OFF_KB + swap_k]], axis=1).astype(bf16)
        gq = jnp.tile(q_gain[l], GQA_Q_HEADS)[None, :]
        gk = jnp.tile(k_gain[l], GQA_KV_HEADS)[None, :]
        gqs = jnp.tile(q_gain[l][_SWAP64], GQA_Q_HEADS)[None, :]
        gks = jnp.tile(k_gain[l][_SWAP64], GQA_KV_HEADS)[None, :]
        wf_bd = _block_diag(w_fourier[l]).astype(bf16)
        w_out_l = w_out[l].astype(bf16)
        wq_l = peer_wq[l].astype(bf16)
        keys_l = peer_subkeys[l].reshape(2 * PEER_HEADS, PEER_NKEYS, PEER_QDIM // 2)
        u_l = peer_u[l].astype(bf16)
        vt_l = peer_v[l].T.astype(bf16)
        bias_l = _na_bias_table(na_rpb[l])
        g1, b1 = ln1_g[l][None, :], ln1_b[l][None, :]
        g2, b2 = ln2_g[l][None, :], ln2_b[l][None, :]

        def ffn(y1, h2, mod_g, tiles_per_batch):
            s1, s2, tau, m1, m2, rz = _peer_query(h2, wq_l, keys_l)
            return _peer(h2, u_l, vt_l, s1, s2, tau, m1, m2, rz, y1, mod_g, g2, b2, tiles_per_batch)

        mod_c = _mod_rows(mod[l], 0, 1)
        tiles_c = n_ctx // ROW_TILE
        qa, ka, va, qb, kb, vb, xc = _inproj(yp, mod_c, w_ctx, bd, gq, gk, None, tiles_c)
        oa, ob = _ctx_attention(qa, ka, va, qb, kb, vb, seq)
        oc = _fourier(xc, wf_bd, batch, seq)
        y1, h2 = _outproj(oa, ob, oc, w_out_l, yp, mod_c, g1, b1, tiles_c)
        yp = ffn(y1, h2, mod_c, n_ctx // PEER_TOK)
        na_k.append(ka.reshape(batch, seq, NA_HEADS, HEAD_DIM))
        na_v.append(va.reshape(batch, seq, NA_HEADS, HEAD_DIM))
        gqa_k.append(kb.reshape(batch, seq, GQA_KV_HEADS, HEAD_DIM))
        gqa_v.append(vb.reshape(batch, seq, GQA_KV_HEADS, HEAD_DIM))

        mod_s = _mod_rows(mod[l], 1, dec_batch)
        tiles_s = dec_seq // ROW_TILE
        qa, ka, va, qb, kb, vb, xc = _inproj(ys, mod_s, w_lat, bd, gq, gk, (gqs, gks, cos_t, sin_t), tiles_s)
        oa = _lat_na_attention(qa, ka, va, cache_na_k[:, l].reshape(dec_batch, past, NA_WIDTH),
                               cache_na_v[:, l].reshape(dec_batch, past, NA_WIDTH), bias_l, dec_batch, dec_seq)
        ob = _lat_gqa_attention(qb, kb, vb, cache_gqa_k[:, l].reshape(dec_batch, past, KV_WIDTH),
                                cache_gqa_v[:, l].reshape(dec_batch, past, KV_WIDTH), dec_batch, dec_seq)
        oc = _fourier(xc, wf_bd, dec_batch, dec_seq)
        y1, h2 = _outproj(oa, ob, oc, w_out_l, ys, mod_s, g1, b1, tiles_s)
        ys = ffn(y1, h2, mod_s, dec_seq // PEER_TOK)

    return (yp.reshape(batch, seq, D_MODEL), ys.reshape(dec_batch, dec_seq, D_MODEL),
            jnp.stack(na_k, axis=1), jnp.stack(na_v, axis=1), jnp.stack(gqa_k, axis=1), jnp.stack(gqa_v, axis=1))
```

```python
import functools
import math

import numpy as np
import jax
import jax.numpy as jnp
from jax import lax
from jax.experimental import pallas as pl
from jax.experimental.pallas import tpu as pltpu

f32 = jnp.float32
bf16 = jnp.bfloat16

D_MODEL = 1024
DEPTH = 2
GRID_W = 64
HEAD_DIM = 64
NA_HEADS = 4
GQA_Q_HEADS = 8
GQA_KV_HEADS = 2
GQA_GROUP = 4
FNET_CH = 64
FNET_GROUPS = 4
NA_WIDTH = 256
GQA_WIDTH = 512
KV_WIDTH = 128
FNET_WIDTH = 256
IN_WIDTH = 1792
OFF_QA, OFF_KA, OFF_VA, OFF_QB, OFF_KB, OFF_VB, OFF_XC = 0, 256, 512, 768, 1280, 1408, 1536
NA_WIN_H = 8
NA_WIN_W = 16
ROPE_THETA = 10000.0
PEER_HEADS = 8
PEER_NKEYS = 128
PEER_EXPERTS = PEER_NKEYS * PEER_NKEYS
PEER_TOPK = 16
PEER_QDIM = 256
ALPHA = (2 * DEPTH) ** 0.25
LN_EPS = 1e-5
RMS_EPS = 1e-6
ATTN_SCALE = HEAD_DIM ** -0.5
INV_SQRT2 = 1.0 / math.sqrt(2.0)
NEG_BIG = -1e30

LANES = 128
SUBLANES = 8
BF16_VREG_LANES = 256
V7X_VMEM_BYTES = 64 * 2**20
V7X_VMEM_LIMIT_CAP = 56 * 2**20

ROW_TILE = 512
PEER_TOK = 512
PEER_ECHUNK = 1024
PEER_IB = PEER_ECHUNK // PEER_NKEYS
PEER_LT = BF16_VREG_LANES
PEER_RB = 64
GQA_QBLK = 256
DFT_ROWS = 256
SCORE_PITCH = 136

NT_DIMS = (((1,), (1,)), ((), ()))
TN_DIMS = (((0,), (0,)), ((), ()))


def _params(sem, est_bytes):
    limit = int(min(V7X_VMEM_LIMIT_CAP, max(16 * 2**20, est_bytes * 5 // 4 + 4 * 2**20)))
    return pltpu.CompilerParams(dimension_semantics=sem, vmem_limit_bytes=limit)


def _layer_norm(z, g, b):
    mu = jnp.mean(z, axis=-1, keepdims=True)
    zc = z - mu
    var = jnp.mean(zc * zc, axis=-1, keepdims=True)
    return zc * lax.rsqrt(var + LN_EPS) * g + b


def _mod_kernel(cond_ref, w_ref, b_ref, o_ref):
    c = cond_ref[...]
    a = (c / (1.0 + jnp.exp(-c))).astype(bf16)
    o_ref[...] = jnp.dot(a, w_ref[...].astype(bf16), preferred_element_type=f32) + b_ref[...]


def _modulation(cond, w_mod, b_mod):
    tn = 1536
    nl = w_mod.shape[0]
    return pl.pallas_call(
        _mod_kernel,
        out_shape=jax.ShapeDtypeStruct((nl, SUBLANES, 6 * D_MODEL), f32),
        grid=(nl, 6 * D_MODEL // tn),
        in_specs=[
            pl.BlockSpec((SUBLANES, D_MODEL), lambda l, j: (0, 0)),
            pl.BlockSpec((None, D_MODEL, tn), lambda l, j: (l, 0, j)),
            pl.BlockSpec((None, 1, tn), lambda l, j: (l, 0, j)),
        ],
        out_specs=pl.BlockSpec((None, SUBLANES, tn), lambda l, j: (l, 0, j)),
        compiler_params=_params(("parallel", "parallel"), 2 * D_MODEL * tn * 4 + D_MODEL * tn * 2),
        name="modulation",
    )(cond, w_mod, b_mod.reshape(nl, 1, 6 * D_MODEL))


def _inproj_kernel(*refs, rope):
    if rope:
        (x_ref, mod_ref, w_ref, bd_ref, gq_ref, gk_ref, gqs_ref, gks_ref, cos_ref, sin_ref,
         qa_ref, ka_ref, va_ref, qb_ref, kb_ref, vb_ref, xc_ref) = refs
    else:
        (x_ref, mod_ref, w_ref, bd_ref, gq_ref, gk_ref,
         qa_ref, ka_ref, va_ref, qb_ref, kb_ref, vb_ref, xc_ref) = refs
    h = (x_ref[...] * (1.0 + mod_ref[1:2, :]) + mod_ref[0:1, :]).astype(bf16)
    p = jnp.dot(h, w_ref[...], preferred_element_type=f32)
    qa_ref[...] = p[:, OFF_QA:OFF_KA]
    ka_ref[...] = p[:, OFF_KA:OFF_VA]
    va_ref[...] = p[:, OFF_VA:OFF_QB]
    vb_ref[...] = p[:, OFF_VB:OFF_XC]
    xc_ref[...] = p[:, OFF_XC:IN_WIDTH].astype(bf16)
    bd = bd_ref[...]

    def normed(off, off_sw, g_ref, gs_ref, c):
        z = p[:, off + c * LANES:off + (c + 1) * LANES]
        r = lax.rsqrt(jnp.dot(z * z, bd, precision=lax.Precision.HIGHEST, preferred_element_type=f32) + RMS_EPS)
        zn = z * r * g_ref[:, c * LANES:(c + 1) * LANES]
        if rope:
            zs = p[:, off_sw + c * LANES:off_sw + (c + 1) * LANES] * r * gs_ref[:, c * LANES:(c + 1) * LANES]
            zn = zn * cos_ref[...] + zs * sin_ref[...]
        return zn

    for c in range(GQA_WIDTH // LANES):
        qb_ref[:, c * LANES:(c + 1) * LANES] = normed(OFF_QB, IN_WIDTH, gq_ref, gqs_ref if rope else None, c)
    kb_ref[...] = normed(OFF_KB, IN_WIDTH + GQA_WIDTH, gk_ref, gks_ref if rope else None, 0)


def _inproj(x, mod, w_ext, bd, gq, gk, rope_args, tiles_per_batch):
    n = x.shape[0]
    rope = rope_args is not None
    w_cols = w_ext.shape[1]
    const = lambda shape: pl.BlockSpec(shape, lambda i: (0,) * len(shape))
    in_specs = [
        pl.BlockSpec((ROW_TILE, D_MODEL), lambda i: (i, 0)),
        pl.BlockSpec((None, 6, D_MODEL), lambda i: (i // tiles_per_batch, 0, 0)),
        const((D_MODEL, w_cols)),
        const((LANES, LANES)),
        const((1, GQA_WIDTH)),
        const((1, KV_WIDTH)),
    ]
    args = [x, mod, w_ext, bd, gq, gk]
    if rope:
        gqs, gks, cos_t, sin_t = rope_args
        in_specs += [
            const((1, GQA_WIDTH)),
            const((1, KV_WIDTH)),
            pl.BlockSpec((ROW_TILE, LANES), lambda i: (i % tiles_per_batch, 0)),
            pl.BlockSpec((ROW_TILE, LANES), lambda i: (i % tiles_per_batch, 0)),
        ]
        args += [gqs, gks, cos_t, sin_t]
    widths = [(NA_WIDTH, f32), (NA_WIDTH, f32), (NA_WIDTH, f32), (GQA_WIDTH, f32), (KV_WIDTH, f32),
              (KV_WIDTH, f32), (FNET_WIDTH, bf16)]
    est = 2 * ROW_TILE * D_MODEL * 4 + 2 * D_MODEL * w_cols * 2 + 3 * ROW_TILE * w_cols * 4
    return pl.pallas_call(
        functools.partial(_inproj_kernel, rope=rope),
        out_shape=[jax.ShapeDtypeStruct((n, w), dt) for w, dt in widths],
        grid=(n // ROW_TILE,),
        in_specs=in_specs,
        out_specs=[pl.BlockSpec((ROW_TILE, w), lambda i: (i, 0)) for w, _ in widths],
        compiler_params=_params(("parallel",), est),
        name="inproj_rope" if rope else "inproj",
    )(*args)


def _scores(q, k):
    return lax.dot_general((q * ATTN_SCALE).astype(bf16), k.astype(bf16), NT_DIMS, preferred_element_type=f32)


def _softmax_pv(s_list, v_list):
    m = s_list[0].max(axis=-1, keepdims=True)
    for s in s_list[1:]:
        m = jnp.maximum(m, s.max(axis=-1, keepdims=True))
    den = None
    out = None
    for s, v in zip(s_list, v_list):
        e = jnp.exp(s - m)
        d = e.sum(axis=-1, keepdims=True)
        o = jnp.dot(e.astype(bf16), v.astype(bf16), preferred_element_type=f32)
        den = d if den is None else den + d
        out = o if out is None else out + o
    return out / den


def _ctx_attn_kernel(qa_ref, ka_ref, va_ref, qb_ref, kb_ref, vb_ref, oa_ref, ob_ref):
    for h in range(NA_HEADS):
        sl = slice(h * HEAD_DIM, (h + 1) * HEAD_DIM)
        oa_ref[:, sl] = _softmax_pv([_scores(qa_ref[:, sl], ka_ref[:, sl])], [va_ref[:, sl]])
    for kv in range(GQA_KV_HEADS):
        ksl = slice(kv * HEAD_DIM, (kv + 1) * HEAD_DIM)
        k = kb_ref[:, ksl]
        v = vb_ref[:, ksl]
        for g in range(GQA_GROUP):
            hd = kv * GQA_GROUP + g
            sl = slice(hd * HEAD_DIM, (hd + 1) * HEAD_DIM)
            ob_ref[:, sl] = _softmax_pv([_scores(qb_ref[:, sl], k)], [v])


def _ctx_attention(qa, ka, va, qb, kb, vb, t):
    n = qa.shape[0]
    spec = lambda w: pl.BlockSpec((t, w), lambda b: (b, 0))
    return pl.pallas_call(
        _ctx_attn_kernel,
        out_shape=[jax.ShapeDtypeStruct((n, NA_WIDTH), f32), jax.ShapeDtypeStruct((n, GQA_WIDTH), f32)],
        grid=(n // t,),
        in_specs=[spec(NA_WIDTH), spec(NA_WIDTH), spec(NA_WIDTH), spec(GQA_WIDTH), spec(KV_WIDTH), spec(KV_WIDTH)],
        out_specs=[spec(NA_WIDTH), spec(GQA_WIDTH)],
        compiler_params=_params(("parallel",), 16 * t * D_MODEL),
        name="ctx_attention",
    )(qa, ka, va, qb, kb, vb)


def _lat_na_kernel(qa_ref, ka_ref, va_ref, ck_ref, cv_ref, bias_ref, oa_ref, *, rows):
    r = pl.program_id(1)
    rs = jnp.clip(r - NA_WIN_H // 2, 0, rows - NA_WIN_H)
    variant = rs - r + (NA_WIN_H - 1)
    start = pl.multiple_of(rs * GRID_W, GRID_W)
    kwin = ka_ref[pl.ds(start, NA_WIN_H * GRID_W), :]
    vwin = va_ref[pl.ds(start, NA_WIN_H * GRID_W), :]
    for h in range(NA_HEADS):
        sl = slice(h * HEAD_DIM, (h + 1) * HEAD_DIM)
        q = qa_ref[:, sl]
        s_win = _scores(q, kwin[:, sl]) + bias_ref[h, variant]
        s_ctx = _scores(q, ck_ref[:, sl])
        oa_ref[:, sl] = _softmax_pv([s_win, s_ctx], [vwin[:, sl], cv_ref[:, sl]])


def _lat_na_attention(qa, ka, va, ck, cv, bias, nb, t):
    rows = t // GRID_W
    past = ck.shape[1]
    n = qa.shape[0]
    return pl.pallas_call(
        functools.partial(_lat_na_kernel, rows=rows),
        out_shape=jax.ShapeDtypeStruct((n, NA_WIDTH), f32),
        grid=(nb, rows),
        in_specs=[
            pl.BlockSpec((GRID_W, NA_WIDTH), lambda b, r: (b * rows + r, 0)),
            pl.BlockSpec((t, NA_WIDTH), lambda b, r: (b, 0)),
            pl.BlockSpec((t, NA_WIDTH), lambda b, r: (b, 0)),
            pl.BlockSpec((None, past, NA_WIDTH), lambda b, r: (b, 0, 0)),
            pl.BlockSpec((None, past, NA_WIDTH), lambda b, r: (b, 0, 0)),
            pl.BlockSpec(bias.shape, lambda b, r: (0, 0, 0, 0)),
        ],
        out_specs=pl.BlockSpec((GRID_W, NA_WIDTH), lambda b, r: (b * rows + r, 0)),
        compiler_params=_params(("parallel", "parallel"), 2 * bias.size * 4 + 8 * t * NA_WIDTH * 4),
        name="latent_na_attention",
    )(qa, ka, va, ck, cv, bias)


def _lat_gqa_kernel(qb_ref, kb_ref, vb_ref, ck_ref, cv_ref, ob_ref):
    for kv in range(GQA_KV_HEADS):
        ksl = slice(kv * HEAD_DIM, (kv + 1) * HEAD_DIM)
        k, v, ck, cv = kb_ref[:, ksl], vb_ref[:, ksl], ck_ref[:, ksl], cv_ref[:, ksl]
        for g in range(GQA_GROUP):
            hd = kv * GQA_GROUP + g
            sl = slice(hd * HEAD_DIM, (hd + 1) * HEAD_DIM)
            q = qb_ref[:, sl]
            ob_ref[:, sl] = _softmax_pv([_scores(q, k), _scores(q, ck)], [v, cv])


def _lat_gqa_attention(qb, kb, vb, ck, cv, nb, t):
    n = qb.shape[0]
    past = ck.shape[1]
    nq = t // GQA_QBLK
    return pl.pallas_call(
        _lat_gqa_kernel,
        out_shape=jax.ShapeDtypeStruct((n, GQA_WIDTH), f32),
        grid=(nb, nq),
        in_specs=[
            pl.BlockSpec((GQA_QBLK, GQA_WIDTH), lambda b, i: (b * nq + i, 0)),
            pl.BlockSpec((t, KV_WIDTH), lambda b, i: (b, 0)),
            pl.BlockSpec((t, KV_WIDTH), lambda b, i: (b, 0)),
            pl.BlockSpec((None, past, KV_WIDTH), lambda b, i: (b, 0, 0)),
            pl.BlockSpec((None, past, KV_WIDTH), lambda b, i: (b, 0, 0)),
        ],
        out_specs=pl.BlockSpec((GQA_QBLK, GQA_WIDTH), lambda b, i: (b * nq + i, 0)),
        compiler_params=_params(("parallel", "parallel"), 6 * GQA_QBLK * (t + past) * 4 + 8 * t * KV_WIDTH * 4),
        name="latent_gqa_attention",
    )(qb, kb, vb, ck, cv)


def _fourier_kernel(ct_ref, st_ref, x_ref, cc_ref, sc_ref, wf_ref, o_ref):
    x = x_ref[...]
    yc = jnp.dot(ct_ref[...], x, preferred_element_type=f32).astype(bf16)
    ys = jnp.dot(st_ref[...], x, preferred_element_type=f32).astype(bf16)
    z = (jnp.dot(yc, cc_ref[...], preferred_element_type=f32)
         - jnp.dot(ys, sc_ref[...], preferred_element_type=f32))
    o_ref[...] = jnp.dot(z.astype(bf16), wf_ref[...], preferred_element_type=f32)


def _dft_tables(t):
    k = np.arange(t, dtype=np.int64)
    ang = 2.0 * np.pi * ((k[:, None] * k[None, :]) % t).astype(np.float64) / t
    return jnp.asarray(np.cos(ang), dtype=f32).astype(bf16), jnp.asarray(np.sin(ang), dtype=f32).astype(bf16)


def _channel_tables(t):
    c = np.arange(FNET_CH, dtype=np.int64)
    ang = 2.0 * np.pi * ((c[:, None] * c[None, :]) % FNET_CH).astype(np.float64) / FNET_CH
    norm = 1.0 / math.sqrt(t * FNET_CH)
    eye = np.eye(FNET_GROUPS)
    return (jnp.asarray(np.kron(eye, np.cos(ang) * norm), dtype=f32).astype(bf16),
            jnp.asarray(np.kron(eye, np.sin(ang) * norm), dtype=f32).astype(bf16))


def _fourier(xc, wf_bd, nb, t):
    n = xc.shape[0]
    rows = min(t, DFT_ROWS)
    nt = t // rows
    ct, st = _dft_tables(t)
    cc, sc = _channel_tables(t)
    const = lambda: pl.BlockSpec((FNET_WIDTH, FNET_WIDTH), lambda b, i: (0, 0))
    return pl.pallas_call(
        _fourier_kernel,
        out_shape=jax.ShapeDtypeStruct((n, FNET_WIDTH), f32),
        grid=(nb, nt),
        in_specs=[
            pl.BlockSpec((rows, t), lambda b, i: (i, 0)),
            pl.BlockSpec((rows, t), lambda b, i: (i, 0)),
            pl.BlockSpec((t, FNET_WIDTH), lambda b, i: (b, 0)),
            const(), const(), const(),
        ],
        out_specs=pl.BlockSpec((rows, FNET_WIDTH), lambda b, i: (b * nt + i, 0)),
        compiler_params=_params(("parallel", "parallel"), 8 * rows * t + 4 * t * FNET_WIDTH),
        name="fourier_mix",
    )(ct, st, xc, cc, sc, wf_bd)


def _outproj_kernel(oa_ref, ob_ref, oc_ref, w_ref, x_ref, mod_ref, g_ref, b_ref, y_ref, h_ref):
    o1, o2 = NA_WIDTH, NA_WIDTH + GQA_WIDTH
    mix = (jnp.dot(oa_ref[...].astype(bf16), w_ref[0:o1, :], preferred_element_type=f32)
           + jnp.dot(ob_ref[...].astype(bf16), w_ref[o1:o2, :], preferred_element_type=f32)
           + jnp.dot(oc_ref[...].astype(bf16), w_ref[o2:D_MODEL, :], preferred_element_type=f32))
    y = _layer_norm(ALPHA * x_ref[...] + mod_ref[2:3, :] * mix, g_ref[...], b_ref[...])
    y_ref[...] = y
    h_ref[...] = (y * (1.0 + mod_ref[4:5, :]) + mod_ref[3:4, :]).astype(bf16)


def _outproj(oa, ob, oc, w_out, x, mod, g, b, tiles_per_batch):
    n = x.shape[0]
    row = lambda w: pl.BlockSpec((ROW_TILE, w), lambda i: (i, 0))
    vec = lambda: pl.BlockSpec((1, D_MODEL), lambda i: (0, 0))
    return pl.pallas_call(
        _outproj_kernel,
        out_shape=[jax.ShapeDtypeStruct((n, D_MODEL), f32), jax.ShapeDtypeStruct((n, D_MODEL), bf16)],
        grid=(n // ROW_TILE,),
        in_specs=[
            row(NA_WIDTH), row(GQA_WIDTH), row(FNET_WIDTH),
            pl.BlockSpec((D_MODEL, D_MODEL), lambda i: (0, 0)),
            row(D_MODEL),
            pl.BlockSpec((None, 6, D_MODEL), lambda i: (i // tiles_per_batch, 0, 0)),
            vec(), vec(),
        ],
        out_specs=[row(D_MODEL), row(D_MODEL)],
        compiler_params=_params(("parallel",), 10 * ROW_TILE * D_MODEL * 4 + 4 * D_MODEL * D_MODEL),
        name="outproj_ln",
    )(oa, ob, oc, w_out, x, mod, g, b)


def _mx(a, b):
    if a is None:
        return b
    if b is None:
        return a
    return jnp.maximum(a, b)


def _mn(a, b):
    if a is None or b is None:
        return None
    return jnp.minimum(a, b)


def _oddeven_pairs(n):
    pairs = []
    p = 1
    while p < n:
        k = p
        while k >= 1:
            for j in range(k % p, n - k, 2 * k):
                for i in range(min(k, n - j - k)):
                    if (i + j) // (2 * p) == (i + j + k) // (2 * p):
                        pairs.append((i + j, i + j + k))
            k //= 2
        p *= 2
    return pairs


_SORT16 = _oddeven_pairs(PEER_TOPK)


def _sort16(v):
    v = list(v)
    for i, j in _SORT16:
        v[i], v[j] = _mx(v[i], v[j]), _mn(v[i], v[j])
    return v


def _merge_top16(a, b):
    c = [_mx(a[k], b[PEER_TOPK - 1 - k]) for k in range(PEER_TOPK)]
    d = PEER_TOPK // 2
    while d >= 1:
        for i in range(PEER_TOPK):
            if i & d == 0:
                c[i], c[i + d] = _mx(c[i], c[i + d]), _mn(c[i], c[i + d])
        d //= 2
    return c


def _top16_sorted(vals):
    lists = [_sort16(vals[i:i + PEER_TOPK]) for i in range(0, len(vals), PEER_TOPK)]
    while len(lists) > 1:
        lists = [_merge_top16(lists[i], lists[i + 1]) for i in range(0, len(lists), 2)]
    return lists[0]


def _kth_pair_sum(a, b):
    cands = {}
    lists = []
    for j in range(PEER_TOPK):
        col = [a[i] + b[j] for i in range(PEER_TOPK // (j + 1))]
        for i, s in enumerate(col):
            cands[(i, j)] = s
        lists.append(col + [None] * (PEER_TOPK - len(col)))
    while len(lists) > 1:
        nxt = [_merge_top16(lists[i], lists[i + 1]) for i in range(0, len(lists) - 1, 2)]
        if len(lists) % 2:
            nxt.append(lists[-1])
        lists = nxt
    return lists[0][PEER_TOPK - 1], cands


def _peer_q_kernel(h_ref, wq_ref, keys_ref, cnt_ref, e1_ref, rho_ref, e2_ref, scr, st_scr):
    q = jnp.dot(h_ref[...], wq_ref[...], preferred_element_type=f32)
    ntile = PEER_TOK // PEER_LT
    sub = PEER_LT // LANES
    for g in range(2 * PEER_HEADS):
        hd, half = divmod(g, 2)
        qg = q[:, g * PEER_NKEYS:(g + 1) * PEER_NKEYS].astype(bf16)
        st = lax.dot_general(keys_ref[g].astype(bf16), qg, NT_DIMS, preferred_element_type=f32)
        base = (half * PEER_HEADS + hd) * SCORE_PITCH
        for tt in range(PEER_TOK // LANES):
            scr[tt, base:base + PEER_NKEYS, :] = st[:, tt * LANES:(tt + 1) * LANES]
        for tt in range(ntile):
            st_scr[g, tt] = st[:, tt * PEER_LT:(tt + 1) * PEER_LT]

    def tile(tt, carry):
        tops = []
        for half in range(2):
            vals = [jnp.concatenate(
                [scr[tt * sub + s, pl.ds(half * PEER_HEADS * SCORE_PITCH + k, PEER_HEADS, stride=SCORE_PITCH), :]
                 for s in range(sub)], axis=1) for k in range(PEER_NKEYS)]
            tops.append(_top16_sorted(vals))
        a, b = tops
        tau, cands = _kth_pair_sum(a, b)
        ea = [jnp.exp(x - a[0]) for x in a]
        eb = [jnp.exp(x - b[0]) for x in b]
        z = None
        for (i, j), s in cands.items():
            term = jnp.where(s >= tau, ea[i] * eb[j], 0.0)
            z = term if z is None else z + term
        rz = 1.0 / z
        col = pl.ds(pl.multiple_of(tt * PEER_LT, PEER_LT), PEER_LT)
        for hd in range(PEER_HEADS):
            s1 = st_scr[2 * hd, tt]
            s2 = st_scr[2 * hd + 1, tt]
            tau_h = tau[hd:hd + 1, :]
            cnt = jnp.zeros((PEER_NKEYS, PEER_LT), f32)
            rho = jnp.zeros((PEER_NKEYS, PEER_LT), f32)
            for k in range(PEER_TOPK):
                bk = b[k][hd:hd + 1, :]
                cnt = cnt + jnp.where(s1 + bk >= tau_h, 1.0, 0.0)
                rho = rho + jnp.where(bk > s2, 1.0, 0.0)
            cnt_ref[hd, :, col] = cnt
            e1_ref[hd, :, col] = jnp.exp(s1 - a[0][hd:hd + 1, :])
            e2 = jnp.exp(s2 - b[0][hd:hd + 1, :]) * rz[hd:hd + 1, :]
            rho_ref[hd, tt] = rho.astype(bf16)
            e2_ref[hd, tt] = e2.astype(bf16)
        return carry

    lax.fori_loop(0, ntile, tile, 0)


def _peer_query(h2, wq, keys):
    n = h2.shape[0]
    qw = PEER_HEADS * PEER_QDIM
    ntile = PEER_TOK // PEER_LT
    row_shape = jax.ShapeDtypeStruct((PEER_HEADS, PEER_NKEYS, n), f32)
    tile_shape = jax.ShapeDtypeStruct((PEER_HEADS, n // PEER_LT, PEER_NKEYS, PEER_LT), bf16)
    row_spec = pl.BlockSpec((PEER_HEADS, PEER_NKEYS, PEER_TOK), lambda i: (0, 0, i))
    tile_spec = pl.BlockSpec((PEER_HEADS, ntile, PEER_NKEYS, PEER_LT), lambda i: (0, i, 0, 0))
    scr_shape = (PEER_TOK // LANES, 2 * PEER_HEADS * SCORE_PITCH, LANES)
    st_shape = (2 * PEER_HEADS, ntile, PEER_NKEYS, PEER_LT)
    est = (2 * PEER_TOK * D_MODEL * 2 + 2 * D_MODEL * qw * 2 + 2 * PEER_TOK * qw * 4
           + 6 * PEER_HEADS * PEER_NKEYS * PEER_TOK * 4 + (int(np.prod(scr_shape)) + int(np.prod(st_shape))) * 4)
    return pl.pallas_call(
        _peer_q_kernel,
        out_shape=[row_shape, row_shape, tile_shape, tile_shape],
        grid=(n // PEER_TOK,),
        in_specs=[
            pl.BlockSpec((PEER_TOK, D_MODEL), lambda i: (i, 0)),
            pl.BlockSpec((D_MODEL, qw), lambda i: (0, 0)),
            pl.BlockSpec((2 * PEER_HEADS, PEER_NKEYS, PEER_QDIM // 2), lambda i: (0, 0, 0)),
        ],
        out_specs=[row_spec, row_spec, tile_spec, tile_spec],
        scratch_shapes=[pltpu.VMEM(scr_shape, f32), pltpu.VMEM(st_shape, f32)],
        compiler_params=_params(("parallel",), est),
        name="peer_query_topk",
    )(h2, wq, keys)


def _peer_kernel(h_ref, u_ref, v_ref, cnt_ref, e1_ref, rho_ref, e2_ref,
                 y_ref, mod_ref, g_ref, b_ref, o_ref, ht_scr, a_scr, w_scr, acc_scr):
    c = pl.program_id(1)

    @pl.when(c == 0)
    def _():
        ht_scr[...] = h_ref[...].T
        acc_scr[...] = jnp.zeros_like(acc_scr)

    a_scr[...] = jnp.dot(u_ref[...], ht_scr[...], preferred_element_type=f32)
    zero = jnp.zeros((), bf16)
    reps = PEER_RB // SUBLANES
    for tt in range(PEER_TOK // PEER_LT):
        col = slice(tt * PEER_LT, (tt + 1) * PEER_LT)
        for ib in range(PEER_IB):
            cbs = [jnp.broadcast_to(cnt_ref[hd, ib:ib + 1, col], (SUBLANES, PEER_LT)).astype(bf16)
                   for hd in range(PEER_HEADS)]
            ebs = [jnp.broadcast_to(e1_ref[hd, ib:ib + 1, col], (SUBLANES, PEER_LT)).astype(bf16)
                   for hd in range(PEER_HEADS)]
            for rb in range(PEER_NKEYS // PEER_RB):
                jrow = slice(rb * PEER_RB, (rb + 1) * PEER_RB)
                gate = jnp.zeros((PEER_RB, PEER_LT), bf16)
                for hd in range(PEER_HEADS):
                    keep = rho_ref[hd, tt, jrow, :] < jnp.concatenate([cbs[hd]] * reps, axis=0)
                    prod = e2_ref[hd, tt, jrow, :] * jnp.concatenate([ebs[hd]] * reps, axis=0)
                    gate = gate + jnp.where(keep, prod, zero)
                row = slice(ib * PEER_NKEYS + rb * PEER_RB, ib * PEER_NKEYS + (rb + 1) * PEER_RB)
                a = a_scr[row, col]
                gelu = (0.5 * a * (1.0 + lax.erf(a * INV_SQRT2))).astype(bf16)
                w_scr[row, col] = gelu * gate
    acc_scr[...] += lax.dot_general(w_scr[...], v_ref[...], TN_DIMS, preferred_element_type=f32)

    @pl.when(c == pl.num_programs(1) - 1)
    def _():
        o_ref[...] = _layer_norm(ALPHA * y_ref[...] + mod_ref[5:6, :] * acc_scr[...], g_ref[...], b_ref[...])


def _peer(h2, u, v, cnt, e1, rho, e2, y1, mod, g, b, tiles_per_batch):
    n = h2.shape[0]
    ntile = PEER_TOK // PEER_LT
    row_spec = pl.BlockSpec((PEER_HEADS, PEER_IB, PEER_TOK), lambda t, c: (0, c, t))
    tile_spec = pl.BlockSpec((PEER_HEADS, ntile, PEER_NKEYS, PEER_LT), lambda t, c: (0, t, 0, 0))
    vec = lambda: pl.BlockSpec((1, D_MODEL), lambda t, c: (0, 0))
    est = (4 * PEER_HEADS * PEER_NKEYS * PEER_TOK * 2 + 4 * PEER_HEADS * PEER_IB * PEER_TOK * 4
           + 4 * PEER_ECHUNK * D_MODEL * 2 + 2 * PEER_TOK * D_MODEL * (2 + 4 + 4)
           + PEER_ECHUNK * PEER_TOK * 6 + D_MODEL * PEER_TOK * 4)
    return pl.pallas_call(
        _peer_kernel,
        out_shape=jax.ShapeDtypeStruct((n, D_MODEL), f32),
        grid=(n // PEER_TOK, PEER_EXPERTS // PEER_ECHUNK),
        in_specs=[
            pl.BlockSpec((PEER_TOK, D_MODEL), lambda t, c: (t, 0)),
            pl.BlockSpec((PEER_ECHUNK, D_MODEL), lambda t, c: (c, 0)),
            pl.BlockSpec((PEER_ECHUNK, D_MODEL), lambda t, c: (c, 0)),
            row_spec, row_spec, tile_spec, tile_spec,
            pl.BlockSpec((PEER_TOK, D_MODEL), lambda t, c: (t, 0)),
            pl.BlockSpec((None, 6, D_MODEL), lambda t, c: (t // tiles_per_batch, 0, 0)),
            vec(), vec(),
        ],
        out_specs=pl.BlockSpec((PEER_TOK, D_MODEL), lambda t, c: (t, 0)),
        scratch_shapes=[
            pltpu.VMEM((D_MODEL, PEER_TOK), bf16),
            pltpu.VMEM((PEER_ECHUNK, PEER_TOK), f32),
            pltpu.VMEM((PEER_ECHUNK, PEER_TOK), bf16),
            pltpu.VMEM((PEER_TOK, D_MODEL), f32),
        ],
        compiler_params=_params(("parallel", "arbitrary"), est),
        name="peer_experts",
    )(h2, u, v, cnt, e1, rho, e2, y1, mod, g, b)


_QUARTER = HEAD_DIM // 4
_SWAP64 = np.concatenate([np.arange(_QUARTER, 2 * _QUARTER), np.arange(0, _QUARTER),
                          np.arange(3 * _QUARTER, 4 * _QUARTER), np.arange(2 * _QUARTER, 3 * _QUARTER)])


def _rope_tables(t):
    pos = jnp.arange(t)
    rows = (pos // GRID_W).astype(f32)
    cols = (pos % GRID_W).astype(f32)
    inv = 1.0 / (ROPE_THETA ** (jnp.arange(_QUARTER, dtype=f32) / _QUARTER))
    ar, ac = rows[:, None] * inv, cols[:, None] * inv
    cos64 = jnp.concatenate([jnp.cos(ar), jnp.cos(ar), jnp.cos(ac), jnp.cos(ac)], axis=-1)
    sin64 = jnp.concatenate([-jnp.sin(ar), jnp.sin(ar), -jnp.sin(ac), jnp.sin(ac)], axis=-1)
    return jnp.tile(cos64, (1, 2)), jnp.tile(sin64, (1, 2))


def _head_mean_matrix():
    return jnp.asarray(np.kron(np.eye(LANES // HEAD_DIM), np.full((HEAD_DIM, HEAD_DIM), 1.0 / HEAD_DIM)), dtype=f32)


def _na_bias_table(rpb):
    w = np.arange(GRID_W)[:, None]
    c = np.arange(GRID_W)[None, :]
    cs = np.clip(w - NA_WIN_W // 2, 0, GRID_W - NA_WIN_W)
    inwin = (c >= cs) & (c < cs + NA_WIN_W)
    coff = np.clip(c - w + (NA_WIN_W - 1), 0, 2 * NA_WIN_W - 2)
    roff = np.arange(NA_WIN_H)[:, None] + np.arange(NA_WIN_H)[None, :]
    t = rpb[:, roff][:, :, :, coff]
    t = jnp.where(inwin[None, None, None], t, NEG_BIG)
    return t.transpose(0, 1, 3, 2, 4).reshape(NA_HEADS, NA_WIN_H, GRID_W, NA_WIN_H * GRID_W)


def _block_diag(w):
    g, c, _ = w.shape
    eye = jnp.eye(g, dtype=w.dtype)
    return (eye[:, None, :, None] * w[:, :, None, :]).reshape(g * c, g * c)


def _mod_rows(mod_l, start, count):
    return mod_l[start:start + count].reshape(count, 6, D_MODEL)


def kernel(x_prompt, x_sample, c, cache_na_k, cache_na_v, cache_gqa_k, cache_gqa_v, c_ctx, w_in, q_gain, k_gain,
           na_rpb, w_fourier, w_out, w_mod, b_mod, ln1_g, ln1_b, ln2_g, ln2_b, peer_wq, peer_subkeys, peer_u, peer_v):
    batch, seq, _ = x_prompt.shape
    dec_batch, dec_seq, _ = x_sample.shape
    past = cache_na_k.shape[2]
    n_ctx, n_lat = batch * seq, dec_batch * dec_seq

    cond = jnp.zeros((SUBLANES, D_MODEL), f32).at[0].set(c_ctx).at[1:1 + dec_batch].set(c)
    mod = _modulation(cond, w_mod, b_mod)

    bd = _head_mean_matrix()
    cos_t, sin_t = _rope_tables(dec_seq)
    swap_q = np.concatenate([h * HEAD_DIM + _SWAP64 for h in range(GQA_Q_HEADS)])
    swap_k = np.concatenate([h * HEAD_DIM + _SWAP64 for h in range(GQA_KV_HEADS)])

    yp = x_prompt.reshape(n_ctx, D_MODEL)
    ys = x_sample.reshape(n_lat, D_MODEL)
    na_k, na_v, gqa_k, gqa_v = [], [], [], []
    for l in range(DEPTH):
        w_in_l = w_in[l]
        w_ctx = w_in_l.astype(bf16)
        w_lat = jnp.concatenate([w_in_l, w_in_l[:, OFF_QB + swap_q], w_in_l[:, OFF_KB + swap_k]], axis=1).astype(bf16)
        gq = jnp.tile(q_gain[l], GQA_Q_HEADS)[None, :]
        gk = jnp.tile(k_gain[l], GQA_KV_HEADS)[None, :]
        gqs = jnp.tile(q_gain[l][_SWAP64], GQA_Q_HEADS)[None, :]
        gks = jnp.tile(k_gain[l][_SWAP64], GQA_KV_HEADS)[None, :]
        wf_bd = _block_diag(w_fourier[l]).astype(bf16)
        w_out_l = w_out[l].astype(bf16)
        wq_l = peer_wq[l].astype(bf16)
        keys_l = peer_subkeys[l].reshape(2 * PEER_HEADS, PEER_NKEYS, PEER_QDIM // 2)
        u_l = peer_u[l].astype(bf16)
        v_l = peer_v[l].astype(bf16)
        bias_l = _na_bias_table(na_rpb[l])
        g1, b1 = ln1_g[l][None, :], ln1_b[l][None, :]
        g2, b2 = ln2_g[l][None, :], ln2_b[l][None, :]

        def ffn(y1, h2, mod_g, tiles_per_batch):
            cnt, e1, rho, e2 = _peer_query(h2, wq_l, keys_l)
            return _peer(h2, u_l, v_l, cnt, e1, rho, e2, y1, mod_g, g2, b2, tiles_per_batch)

        mod_c = _mod_rows(mod[l], 0, 1)
        tiles_c = n_ctx // ROW_TILE
        qa, ka, va, qb, kb, vb, xc = _inproj(yp, mod_c, w_ctx, bd, gq, gk, None, tiles_c)
        oa, ob = _ctx_attention(qa, ka, va, qb, kb, vb, seq)
        oc = _fourier(xc, wf_bd, batch, seq)
        y1, h2 = _outproj(oa, ob, oc, w_out_l, yp, mod_c, g1, b1, tiles_c)
        yp = ffn(y1, h2, mod_c, n_ctx // PEER_TOK)
        na_k.append(ka.reshape(batch, seq, NA_HEADS, HEAD_DIM))
        na_v.append(va.reshape(batch, seq, NA_HEADS, HEAD_DIM))
        gqa_k.append(kb.reshape(batch, seq, GQA_KV_HEADS, HEAD_DIM))
        gqa_v.append(vb.reshape(batch, seq, GQA_KV_HEADS, HEAD_DIM))

        mod_s = _mod_rows(mod[l], 1, dec_batch)
        tiles_s = dec_seq // ROW_TILE
        qa, ka, va, qb, kb, vb, xc = _inproj(ys, mod_s, w_lat, bd, gq, gk, (gqs, gks, cos_t, sin_t), tiles_s)
        oa = _lat_na_attention(qa, ka, va, cache_na_k[:, l].reshape(dec_batch, past, NA_WIDTH),
                               cache_na_v[:, l].reshape(dec_batch, past, NA_WIDTH), bias_l, dec_batch, dec_seq)
        ob = _lat_gqa_attention(qb, kb, vb, cache_gqa_k[:, l].reshape(dec_batch, past, KV_WIDTH),
                                cache_gqa_v[:, l].reshape(dec_batch, past, KV_WIDTH), dec_batch, dec_seq)
        oc = _fourier(xc, wf_bd, dec_batch, dec_seq)
        y1, h2 = _outproj(oa, ob, oc, w_out_l, ys, mod_s, g1, b1, tiles_s)
        ys = ffn(y1, h2, mod_s, dec_seq // PEER_TOK)

    return (yp.reshape(batch, seq, D_MODEL), ys.reshape(dec_batch, dec_seq, D_MODEL),
            jnp.stack(na_k, axis=1), jnp.stack(na_v, axis=1), jnp.stack(gqa_k, axis=1), jnp.stack(gqa_v, axis=1))
```

```python
import functools
import math

import numpy as np
import jax
import jax.numpy as jnp
from jax import lax
from jax.experimental import pallas as pl
from jax.experimental.pallas import tpu as pltpu

f32 = jnp.float32
bf16 = jnp.bfloat16

D_MODEL = 1024
DEPTH = 2
GRID_W = 64
HEAD_DIM = 64
NA_HEADS = 4
GQA_Q_HEADS = 8
GQA_KV_HEADS = 2
GQA_GROUP = 4
FNET_CH = 64
FNET_GROUPS = 4
NA_WIDTH = 256
GQA_WIDTH = 512
KV_WIDTH = 128
FNET_WIDTH = 256
IN_WIDTH = 1792
OFF_QA, OFF_KA, OFF_VA, OFF_QB, OFF_KB, OFF_VB, OFF_XC = 0, 256, 512, 768, 1280, 1408, 1536
NA_WIN_H = 8
NA_WIN_W = 16
ROPE_THETA = 10000.0
PEER_HEADS = 8
PEER_NKEYS = 128
PEER_EXPERTS = PEER_NKEYS * PEER_NKEYS
PEER_TOPK = 16
PEER_QDIM = 256
ALPHA = (2 * DEPTH) ** 0.25
LN_EPS = 1e-5
RMS_EPS = 1e-6
ATTN_SCALE = HEAD_DIM ** -0.5
INV_SQRT2 = 1.0 / math.sqrt(2.0)
NEG_BIG = -1e30

LANES = 128
SUBLANES = 8
BF16_VREG_LANES = 256
V7X_VMEM_BYTES = 64 * 2**20
V7X_VMEM_LIMIT_CAP = 56 * 2**20

ROW_TILE = 512
PEERQ_TOK = 512
PEER_TOK = 1024
PEER_ECHUNK = 1024
PEER_IB = PEER_ECHUNK // PEER_NKEYS
PEER_LT = BF16_VREG_LANES
PEER_RB = 64
CNT_DIRECT = 4
GQA_QBLK = 256
NA_ROWS = 4
DFT_ROWS = 256
SCORE_PITCH = 136

NT_DIMS = (((1,), (1,)), ((), ()))
TN_DIMS = (((0,), (0,)), ((), ()))


def _params(sem, est_bytes):
    limit = int(min(V7X_VMEM_LIMIT_CAP, max(16 * 2**20, est_bytes * 5 // 4 + 4 * 2**20)))
    return pltpu.CompilerParams(dimension_semantics=sem, vmem_limit_bytes=limit)


def _layer_norm(z, g, b):
    mu = jnp.mean(z, axis=-1, keepdims=True)
    zc = z - mu
    var = jnp.mean(zc * zc, axis=-1, keepdims=True)
    return zc * lax.rsqrt(var + LN_EPS) * g + b


def _mod_kernel(cond_ref, w_ref, b_ref, o_ref):
    c = cond_ref[...]
    a = (c / (1.0 + jnp.exp(-c))).astype(bf16)
    o_ref[...] = jnp.dot(a, w_ref[...].astype(bf16), preferred_element_type=f32) + b_ref[...]


def _modulation(cond, w_mod, b_mod):
    tn = 1536
    nl = w_mod.shape[0]
    return pl.pallas_call(
        _mod_kernel,
        out_shape=jax.ShapeDtypeStruct((nl, SUBLANES, 6 * D_MODEL), f32),
        grid=(nl, 6 * D_MODEL // tn),
        in_specs=[
            pl.BlockSpec((SUBLANES, D_MODEL), lambda l, j: (0, 0)),
            pl.BlockSpec((None, D_MODEL, tn), lambda l, j: (l, 0, j)),
            pl.BlockSpec((None, 1, tn), lambda l, j: (l, 0, j)),
        ],
        out_specs=pl.BlockSpec((None, SUBLANES, tn), lambda l, j: (l, 0, j)),
        compiler_params=_params(("parallel", "parallel"), 2 * D_MODEL * tn * 4 + D_MODEL * tn * 2),
        name="modulation",
    )(cond, w_mod, b_mod.reshape(nl, 1, 6 * D_MODEL))


def _inproj_kernel(*refs, rope):
    if rope:
        (x_ref, mod_ref, w_ref, bd_ref, gq_ref, gk_ref, gqs_ref, gks_ref, cos_ref, sin_ref,
         qa_ref, ka_ref, va_ref, qb_ref, kb_ref, vb_ref, xc_ref) = refs
    else:
        (x_ref, mod_ref, w_ref, bd_ref, gq_ref, gk_ref,
         qa_ref, ka_ref, va_ref, qb_ref, kb_ref, vb_ref, xc_ref) = refs
    h = (x_ref[...] * (1.0 + mod_ref[1:2, :]) + mod_ref[0:1, :]).astype(bf16)
    p = jnp.dot(h, w_ref[...], preferred_element_type=f32)
    qa_ref[...] = p[:, OFF_QA:OFF_KA]
    ka_ref[...] = p[:, OFF_KA:OFF_VA]
    va_ref[...] = p[:, OFF_VA:OFF_QB]
    vb_ref[...] = p[:, OFF_VB:OFF_XC]
    xc_ref[...] = p[:, OFF_XC:IN_WIDTH].astype(bf16)
    bd = bd_ref[...]

    def normed(off, off_sw, g_ref, gs_ref, c):
        z = p[:, off + c * LANES:off + (c + 1) * LANES]
        r = lax.rsqrt(jnp.dot(z * z, bd, precision=lax.Precision.HIGHEST, preferred_element_type=f32) + RMS_EPS)
        zn = z * r * g_ref[:, c * LANES:(c + 1) * LANES]
        if rope:
            zs = p[:, off_sw + c * LANES:off_sw + (c + 1) * LANES] * r * gs_ref[:, c * LANES:(c + 1) * LANES]
            zn = zn * cos_ref[...] + zs * sin_ref[...]
        return zn

    for c in range(GQA_WIDTH // LANES):
        qb_ref[:, c * LANES:(c + 1) * LANES] = normed(OFF_QB, IN_WIDTH, gq_ref, gqs_ref if rope else None, c)
    kb_ref[...] = normed(OFF_KB, IN_WIDTH + GQA_WIDTH, gk_ref, gks_ref if rope else None, 0)


def _inproj(x, mod, w_ext, bd, gq, gk, rope_args, tiles_per_batch):
    n = x.shape[0]
    rope = rope_args is not None
    w_cols = w_ext.shape[1]
    const = lambda shape: pl.BlockSpec(shape, lambda i: (0,) * len(shape))
    in_specs = [
        pl.BlockSpec((ROW_TILE, D_MODEL), lambda i: (i, 0)),
        pl.BlockSpec((None, 6, D_MODEL), lambda i: (i // tiles_per_batch, 0, 0)),
        const((D_MODEL, w_cols)),
        const((LANES, LANES)),
        const((1, GQA_WIDTH)),
        const((1, KV_WIDTH)),
    ]
    args = [x, mod, w_ext, bd, gq, gk]
    if rope:
        gqs, gks, cos_t, sin_t = rope_args
        in_specs += [
            const((1, GQA_WIDTH)),
            const((1, KV_WIDTH)),
            pl.BlockSpec((ROW_TILE, LANES), lambda i: (i % tiles_per_batch, 0)),
            pl.BlockSpec((ROW_TILE, LANES), lambda i: (i % tiles_per_batch, 0)),
        ]
        args += [gqs, gks, cos_t, sin_t]
    widths = [(NA_WIDTH, f32), (NA_WIDTH, f32), (NA_WIDTH, f32), (GQA_WIDTH, f32), (KV_WIDTH, f32),
              (KV_WIDTH, f32), (FNET_WIDTH, bf16)]
    est = 2 * ROW_TILE * D_MODEL * 4 + 2 * D_MODEL * w_cols * 2 + 3 * ROW_TILE * w_cols * 4
    return pl.pallas_call(
        functools.partial(_inproj_kernel, rope=rope),
        out_shape=[jax.ShapeDtypeStruct((n, w), dt) for w, dt in widths],
        grid=(n // ROW_TILE,),
        in_specs=in_specs,
        out_specs=[pl.BlockSpec((ROW_TILE, w), lambda i: (i, 0)) for w, _ in widths],
        compiler_params=_params(("parallel",), est),
        name="inproj_rope" if rope else "inproj",
    )(*args)


def _scores(q, k):
    return lax.dot_general((q * ATTN_SCALE).astype(bf16), k.astype(bf16), NT_DIMS, preferred_element_type=f32)


def _softmax_pv(s_list, v_list):
    m = s_list[0].max(axis=-1, keepdims=True)
    for s in s_list[1:]:
        m = jnp.maximum(m, s.max(axis=-1, keepdims=True))
    den = None
    out = None
    for s, v in zip(s_list, v_list):
        e = jnp.exp(s - m)
        d = e.sum(axis=-1, keepdims=True)
        o = jnp.dot(e.astype(bf16), v.astype(bf16), preferred_element_type=f32)
        den = d if den is None else den + d
        out = o if out is None else out + o
    return out / den


def _ctx_attn_kernel(qa_ref, ka_ref, va_ref, qb_ref, kb_ref, vb_ref, oa_ref, ob_ref):
    for h in range(NA_HEADS):
        sl = slice(h * HEAD_DIM, (h + 1) * HEAD_DIM)
        oa_ref[:, sl] = _softmax_pv([_scores(qa_ref[:, sl], ka_ref[:, sl])], [va_ref[:, sl]])
    for kv in range(GQA_KV_HEADS):
        ksl = slice(kv * HEAD_DIM, (kv + 1) * HEAD_DIM)
        k = kb_ref[:, ksl]
        v = vb_ref[:, ksl]
        for g in range(GQA_GROUP):
            hd = kv * GQA_GROUP + g
            sl = slice(hd * HEAD_DIM, (hd + 1) * HEAD_DIM)
            ob_ref[:, sl] = _softmax_pv([_scores(qb_ref[:, sl], k)], [v])


def _ctx_attention(qa, ka, va, qb, kb, vb, t):
    n = qa.shape[0]
    spec = lambda w: pl.BlockSpec((t, w), lambda b: (b, 0))
    return pl.pallas_call(
        _ctx_attn_kernel,
        out_shape=[jax.ShapeDtypeStruct((n, NA_WIDTH), f32), jax.ShapeDtypeStruct((n, GQA_WIDTH), f32)],
        grid=(n // t,),
        in_specs=[spec(NA_WIDTH), spec(NA_WIDTH), spec(NA_WIDTH), spec(GQA_WIDTH), spec(KV_WIDTH), spec(KV_WIDTH)],
        out_specs=[spec(NA_WIDTH), spec(GQA_WIDTH)],
        compiler_params=_params(("parallel",), 16 * t * D_MODEL),
        name="ctx_attention",
    )(qa, ka, va, qb, kb, vb)


def _lat_na_kernel(qa_ref, ka_ref, va_ref, ck_ref, cv_ref, bias_ref, oa_ref, *, rows):
    for rr in range(NA_ROWS):
        r = pl.program_id(1) * NA_ROWS + rr
        rs = jnp.clip(r - NA_WIN_H // 2, 0, rows - NA_WIN_H)
        variant = rs - r + (NA_WIN_H - 1)
        start = pl.multiple_of(rs * GRID_W, GRID_W)
        kwin = ka_ref[pl.ds(start, NA_WIN_H * GRID_W), :]
        vwin = va_ref[pl.ds(start, NA_WIN_H * GRID_W), :]
        qrow = slice(rr * GRID_W, (rr + 1) * GRID_W)
        for h in range(NA_HEADS):
            sl = slice(h * HEAD_DIM, (h + 1) * HEAD_DIM)
            q = qa_ref[qrow, sl]
            s_win = _scores(q, kwin[:, sl]) + bias_ref[h, variant]
            s_ctx = _scores(q, ck_ref[:, sl])
            oa_ref[qrow, sl] = _softmax_pv([s_win, s_ctx], [vwin[:, sl], cv_ref[:, sl]])


def _lat_na_attention(qa, ka, va, ck, cv, bias, nb, t):
    rows = t // GRID_W
    past = ck.shape[1]
    n = qa.shape[0]
    nblk = rows // NA_ROWS
    qrows = NA_ROWS * GRID_W
    return pl.pallas_call(
        functools.partial(_lat_na_kernel, rows=rows),
        out_shape=jax.ShapeDtypeStruct((n, NA_WIDTH), f32),
        grid=(nb, nblk),
        in_specs=[
            pl.BlockSpec((qrows, NA_WIDTH), lambda b, r: (b * nblk + r, 0)),
            pl.BlockSpec((t, NA_WIDTH), lambda b, r: (b, 0)),
            pl.BlockSpec((t, NA_WIDTH), lambda b, r: (b, 0)),
            pl.BlockSpec((None, past, NA_WIDTH), lambda b, r: (b, 0, 0)),
            pl.BlockSpec((None, past, NA_WIDTH), lambda b, r: (b, 0, 0)),
            pl.BlockSpec(bias.shape, lambda b, r: (0, 0, 0, 0)),
        ],
        out_specs=pl.BlockSpec((qrows, NA_WIDTH), lambda b, r: (b * nblk + r, 0)),
        compiler_params=_params(("parallel", "parallel"), 2 * bias.size * 4 + 8 * t * NA_WIDTH * 4),
        name="latent_na_attention",
    )(qa, ka, va, ck, cv, bias)


def _lat_gqa_kernel(qb_ref, kb_ref, vb_ref, ck_ref, cv_ref, ob_ref):
    for kv in range(GQA_KV_HEADS):
        ksl = slice(kv * HEAD_DIM, (kv + 1) * HEAD_DIM)
        k, v, ck, cv = kb_ref[:, ksl], vb_ref[:, ksl], ck_ref[:, ksl], cv_ref[:, ksl]
        for g in range(GQA_GROUP):
            hd = kv * GQA_GROUP + g
            sl = slice(hd * HEAD_DIM, (hd + 1) * HEAD_DIM)
            q = qb_ref[:, sl]
            ob_ref[:, sl] = _softmax_pv([_scores(q, k), _scores(q, ck)], [v, cv])


def _lat_gqa_attention(qb, kb, vb, ck, cv, nb, t):
    n = qb.shape[0]
    past = ck.shape[1]
    nq = t // GQA_QBLK
    return pl.pallas_call(
        _lat_gqa_kernel,
        out_shape=jax.ShapeDtypeStruct((n, GQA_WIDTH), f32),
        grid=(nb, nq),
        in_specs=[
            pl.BlockSpec((GQA_QBLK, GQA_WIDTH), lambda b, i: (b * nq + i, 0)),
            pl.BlockSpec((t, KV_WIDTH), lambda b, i: (b, 0)),
            pl.BlockSpec((t, KV_WIDTH), lambda b, i: (b, 0)),
            pl.BlockSpec((None, past, KV_WIDTH), lambda b, i: (b, 0, 0)),
            pl.BlockSpec((None, past, KV_WIDTH), lambda b, i: (b, 0, 0)),
        ],
        out_specs=pl.BlockSpec((GQA_QBLK, GQA_WIDTH), lambda b, i: (b * nq + i, 0)),
        compiler_params=_params(("parallel", "parallel"), 6 * GQA_QBLK * (t + past) * 4 + 8 * t * KV_WIDTH * 4),
        name="latent_gqa_attention",
    )(qb, kb, vb, ck, cv)


def _fourier_kernel(ct_ref, st_ref, x_ref, cc_ref, sc_ref, wf_ref, o_ref):
    x = x_ref[...]
    yc = jnp.dot(ct_ref[...], x, preferred_element_type=f32).astype(bf16)
    ys = jnp.dot(st_ref[...], x, preferred_element_type=f32).astype(bf16)
    z = (jnp.dot(yc, cc_ref[...], preferred_element_type=f32)
         - jnp.dot(ys, sc_ref[...], preferred_element_type=f32))
    o_ref[...] = jnp.dot(z.astype(bf16), wf_ref[...], preferred_element_type=f32)


def _dft_tables(t):
    k = np.arange(t, dtype=np.int64)
    ang = 2.0 * np.pi * ((k[:, None] * k[None, :]) % t).astype(np.float64) / t
    return jnp.asarray(np.cos(ang), dtype=f32).astype(bf16), jnp.asarray(np.sin(ang), dtype=f32).astype(bf16)


def _channel_tables(t):
    c = np.arange(FNET_CH, dtype=np.int64)
    ang = 2.0 * np.pi * ((c[:, None] * c[None, :]) % FNET_CH).astype(np.float64) / FNET_CH
    norm = 1.0 / math.sqrt(t * FNET_CH)
    eye = np.eye(FNET_GROUPS)
    return (jnp.asarray(np.kron(eye, np.cos(ang) * norm), dtype=f32).astype(bf16),
            jnp.asarray(np.kron(eye, np.sin(ang) * norm), dtype=f32).astype(bf16))


def _fourier(xc, wf_bd, nb, t):
    n = xc.shape[0]
    rows = min(t, DFT_ROWS)
    nt = t // rows
    ct, st = _dft_tables(t)
    cc, sc = _channel_tables(t)
    const = lambda: pl.BlockSpec((FNET_WIDTH, FNET_WIDTH), lambda b, i: (0, 0))
    return pl.pallas_call(
        _fourier_kernel,
        out_shape=jax.ShapeDtypeStruct((n, FNET_WIDTH), f32),
        grid=(nb, nt),
        in_specs=[
            pl.BlockSpec((rows, t), lambda b, i: (i, 0)),
            pl.BlockSpec((rows, t), lambda b, i: (i, 0)),
            pl.BlockSpec((t, FNET_WIDTH), lambda b, i: (b, 0)),
            const(), const(), const(),
        ],
        out_specs=pl.BlockSpec((rows, FNET_WIDTH), lambda b, i: (b * nt + i, 0)),
        compiler_params=_params(("parallel", "parallel"), 8 * rows * t + 4 * t * FNET_WIDTH),
        name="fourier_mix",
    )(ct, st, xc, cc, sc, wf_bd)


def _outproj_kernel(oa_ref, ob_ref, oc_ref, w_ref, x_ref, mod_ref, g_ref, b_ref, y_ref, h_ref):
    o1, o2 = NA_WIDTH, NA_WIDTH + GQA_WIDTH
    mix = (jnp.dot(oa_ref[...].astype(bf16), w_ref[0:o1, :], preferred_element_type=f32)
           + jnp.dot(ob_ref[...].astype(bf16), w_ref[o1:o2, :], preferred_element_type=f32)
           + jnp.dot(oc_ref[...].astype(bf16), w_ref[o2:D_MODEL, :], preferred_element_type=f32))
    y = _layer_norm(ALPHA * x_ref[...] + mod_ref[2:3, :] * mix, g_ref[...], b_ref[...])
    y_ref[...] = y
    h_ref[...] = (y * (1.0 + mod_ref[4:5, :]) + mod_ref[3:4, :]).astype(bf16)


def _outproj(oa, ob, oc, w_out, x, mod, g, b, tiles_per_batch):
    n = x.shape[0]
    row = lambda w: pl.BlockSpec((ROW_TILE, w), lambda i: (i, 0))
    vec = lambda: pl.BlockSpec((1, D_MODEL), lambda i: (0, 0))
    return pl.pallas_call(
        _outproj_kernel,
        out_shape=[jax.ShapeDtypeStruct((n, D_MODEL), f32), jax.ShapeDtypeStruct((n, D_MODEL), bf16)],
        grid=(n // ROW_TILE,),
        in_specs=[
            row(NA_WIDTH), row(GQA_WIDTH), row(FNET_WIDTH),
            pl.BlockSpec((D_MODEL, D_MODEL), lambda i: (0, 0)),
            row(D_MODEL),
            pl.BlockSpec((None, 6, D_MODEL), lambda i: (i // tiles_per_batch, 0, 0)),
            vec(), vec(),
        ],
        out_specs=[row(D_MODEL), row(D_MODEL)],
        compiler_params=_params(("parallel",), 10 * ROW_TILE * D_MODEL * 4 + 4 * D_MODEL * D_MODEL),
        name="outproj_ln",
    )(oa, ob, oc, w_out, x, mod, g, b)


def _mx(a, b):
    if a is None:
        return b
    if b is None:
        return a
    return jnp.maximum(a, b)


def _mn(a, b):
    if a is None or b is None:
        return None
    return jnp.minimum(a, b)


def _oddeven_pairs(n):
    pairs = []
    p = 1
    while p < n:
        k = p
        while k >= 1:
            for j in range(k % p, n - k, 2 * k):
                for i in range(min(k, n - j - k)):
                    if (i + j) // (2 * p) == (i + j + k) // (2 * p):
                        pairs.append((i + j, i + j + k))
            k //= 2
        p *= 2
    return pairs


_SORT16 = _oddeven_pairs(PEER_TOPK)


def _sort16(v):
    v = list(v)
    for i, j in _SORT16:
        v[i], v[j] = _mx(v[i], v[j]), _mn(v[i], v[j])
    return v


def _merge_top16(a, b):
    c = [_mx(a[k], b[PEER_TOPK - 1 - k]) for k in range(PEER_TOPK)]
    d = PEER_TOPK // 2
    while d >= 1:
        for i in range(PEER_TOPK):
            if i & d == 0:
                c[i], c[i + d] = _mx(c[i], c[i + d]), _mn(c[i], c[i + d])
        d //= 2
    return c


def _top16_sorted(vals):
    lists = [_sort16(vals[i:i + PEER_TOPK]) for i in range(0, len(vals), PEER_TOPK)]
    while len(lists) > 1:
        lists = [_merge_top16(lists[i], lists[i + 1]) for i in range(0, len(lists), 2)]
    return lists[0]


def _kth_pair_sum(a, b):
    cands = {}
    lists = []
    for j in range(PEER_TOPK):
        col = [a[i] + b[j] for i in range(PEER_TOPK // (j + 1))]
        for i, s in enumerate(col):
            cands[(i, j)] = s
        lists.append(col + [None] * (PEER_TOPK - len(col)))
    while len(lists) > 1:
        nxt = [_merge_top16(lists[i], lists[i + 1]) for i in range(0, len(lists) - 1, 2)]
        if len(lists) % 2:
            nxt.append(lists[-1])
        lists = nxt
    return lists[0][PEER_TOPK - 1], cands


def _peer_q_kernel(h_ref, wq_ref, keys_ref, cnt_ref, e1_ref, rho_ref, e2_ref, scr, st_scr):
    q = jnp.dot(h_ref[...], wq_ref[...], preferred_element_type=f32)
    ntile = PEERQ_TOK // PEER_LT
    sub = PEER_LT // LANES
    for g in range(2 * PEER_HEADS):
        hd, half = divmod(g, 2)
        qg = q[:, g * PEER_NKEYS:(g + 1) * PEER_NKEYS].astype(bf16)
        st = lax.dot_general(keys_ref[g].astype(bf16), qg, NT_DIMS, preferred_element_type=f32)
        base = (half * PEER_HEADS + hd) * SCORE_PITCH
        for tt in range(PEERQ_TOK // LANES):
            scr[tt, base:base + PEER_NKEYS, :] = st[:, tt * LANES:(tt + 1) * LANES]
        for tt in range(ntile):
            st_scr[g, tt] = st[:, tt * PEER_LT:(tt + 1) * PEER_LT]

    def tile(tt, carry):
        tops = []
        for half in range(2):
            vals = [jnp.concatenate(
                [scr[tt * sub + s, pl.ds(half * PEER_HEADS * SCORE_PITCH + k, PEER_HEADS, stride=SCORE_PITCH), :]
                 for s in range(sub)], axis=1) for k in range(PEER_NKEYS)]
            tops.append(_top16_sorted(vals))
        a, b = tops
        tau, cands = _kth_pair_sum(a, b)
        ea = [jnp.exp(x - a[0]) for x in a]
        eb = [jnp.exp(x - b[0]) for x in b]
        z = None
        for (i, j), s in cands.items():
            term = jnp.where(s >= tau, ea[i] * eb[j], 0.0)
            z = term if z is None else z + term
        rz = 1.0 / z
        extras = []
        for m in range(PEER_TOPK // (CNT_DIRECT + 1)):
            ex = None
            for k in range(CNT_DIRECT, PEER_TOPK // (m + 1)):
                t = jnp.where(a[m] + b[k] >= tau, 1.0, 0.0)
                ex = t if ex is None else ex + t
            extras.append(ex)
        col = pl.ds(pl.multiple_of(tt * PEER_LT, PEER_LT), PEER_LT)
        for hd in range(PEER_HEADS):
            s1 = st_scr[2 * hd, tt]
            s2 = st_scr[2 * hd + 1, tt]
            tau_h = tau[hd:hd + 1, :]
            cnt = jnp.zeros((PEER_NKEYS, PEER_LT), f32)
            rho = jnp.zeros((PEER_NKEYS, PEER_LT), f32)
            for k in range(CNT_DIRECT):
                cnt = cnt + jnp.where(s1 + b[k][hd:hd + 1, :] >= tau_h, 1.0, 0.0)
            for m, ex in enumerate(extras):
                cnt = cnt + jnp.where(s1 == a[m][hd:hd + 1, :], ex[hd:hd + 1, :], 0.0)
            for k in range(PEER_TOPK):
                rho = rho + jnp.where(b[k][hd:hd + 1, :] > s2, 1.0, 0.0)
            cnt_ref[hd, :, col] = cnt
            e1_ref[hd, :, col] = jnp.exp(s1 - a[0][hd:hd + 1, :])
            e2 = jnp.exp(s2 - b[0][hd:hd + 1, :]) * rz[hd:hd + 1, :]
            rho_ref[hd, tt] = rho.astype(bf16)
            e2_ref[hd, tt] = e2.astype(bf16)
        return carry

    lax.fori_loop(0, ntile, tile, 0)


def _peer_query(h2, wq, keys):
    n = h2.shape[0]
    qw = PEER_HEADS * PEER_QDIM
    ntile = PEERQ_TOK // PEER_LT
    row_shape = jax.ShapeDtypeStruct((PEER_HEADS, PEER_NKEYS, n), f32)
    tile_shape = jax.ShapeDtypeStruct((PEER_HEADS, n // PEER_LT, PEER_NKEYS, PEER_LT), bf16)
    row_spec = pl.BlockSpec((PEER_HEADS, PEER_NKEYS, PEERQ_TOK), lambda i: (0, 0, i))
    tile_spec = pl.BlockSpec((PEER_HEADS, ntile, PEER_NKEYS, PEER_LT), lambda i: (0, i, 0, 0))
    scr_shape = (PEERQ_TOK // LANES, 2 * PEER_HEADS * SCORE_PITCH, LANES)
    st_shape = (2 * PEER_HEADS, ntile, PEER_NKEYS, PEER_LT)
    est = (2 * PEERQ_TOK * D_MODEL * 2 + 2 * D_MODEL * qw * 2 + 2 * PEERQ_TOK * qw * 4
           + 6 * PEER_HEADS * PEER_NKEYS * PEERQ_TOK * 4 + (int(np.prod(scr_shape)) + int(np.prod(st_shape))) * 4)
    return pl.pallas_call(
        _peer_q_kernel,
        out_shape=[row_shape, row_shape, tile_shape, tile_shape],
        grid=(n // PEERQ_TOK,),
        in_specs=[
            pl.BlockSpec((PEERQ_TOK, D_MODEL), lambda i: (i, 0)),
            pl.BlockSpec((D_MODEL, qw), lambda i: (0, 0)),
            pl.BlockSpec((2 * PEER_HEADS, PEER_NKEYS, PEER_QDIM // 2), lambda i: (0, 0, 0)),
        ],
        out_specs=[row_spec, row_spec, tile_spec, tile_spec],
        scratch_shapes=[pltpu.VMEM(scr_shape, f32), pltpu.VMEM(st_shape, f32)],
        compiler_params=_params(("parallel",), est),
        name="peer_query_topk",
    )(h2, wq, keys)


def _peer_kernel(h_ref, u_ref, v_ref, cnt_ref, e1_ref, rho_ref, e2_ref,
                 y_ref, mod_ref, g_ref, b_ref, o_ref, ht_scr, a_scr, w_scr, acc_scr):
    c = pl.program_id(1)

    @pl.when(c == 0)
    def _():
        ht_scr[...] = h_ref[...].T
        acc_scr[...] = jnp.zeros_like(acc_scr)

    a_scr[...] = jnp.dot(u_ref[...], ht_scr[...], preferred_element_type=f32)
    zero = jnp.zeros((), bf16)
    reps = PEER_RB // SUBLANES
    for tt in range(PEER_TOK // PEER_LT):
        col = slice(tt * PEER_LT, (tt + 1) * PEER_LT)
        for ib in range(PEER_IB):
            cbs = [jnp.broadcast_to(cnt_ref[hd, ib:ib + 1, col], (SUBLANES, PEER_LT)).astype(bf16)
                   for hd in range(PEER_HEADS)]
            ebs = [jnp.broadcast_to(e1_ref[hd, ib:ib + 1, col], (SUBLANES, PEER_LT)).astype(bf16)
                   for hd in range(PEER_HEADS)]
            for rb in range(PEER_NKEYS // PEER_RB):
                jrow = slice(rb * PEER_RB, (rb + 1) * PEER_RB)
                gate = jnp.zeros((PEER_RB, PEER_LT), bf16)
                for hd in range(PEER_HEADS):
                    keep = rho_ref[hd, tt, jrow, :] < jnp.concatenate([cbs[hd]] * reps, axis=0)
                    prod = e2_ref[hd, tt, jrow, :] * jnp.concatenate([ebs[hd]] * reps, axis=0)
                    gate = gate + jnp.where(keep, prod, zero)
                row = slice(ib * PEER_NKEYS + rb * PEER_RB, ib * PEER_NKEYS + (rb + 1) * PEER_RB)
                a = a_scr[row, col]
                gelu = (0.5 * a * (1.0 + lax.erf(a * INV_SQRT2))).astype(bf16)
                w_scr[row, col] = gelu * gate
    acc_scr[...] += lax.dot_general(w_scr[...], v_ref[...], TN_DIMS, preferred_element_type=f32)

    @pl.when(c == pl.num_programs(1) - 1)
    def _():
        o_ref[...] = _layer_norm(ALPHA * y_ref[...] + mod_ref[5:6, :] * acc_scr[...], g_ref[...], b_ref[...])


def _peer(h2, u, v, cnt, e1, rho, e2, y1, mod, g, b, tiles_per_batch):
    n = h2.shape[0]
    ntile = PEER_TOK // PEER_LT
    row_spec = pl.BlockSpec((PEER_HEADS, PEER_IB, PEER_TOK), lambda t, c: (0, c, t))
    tile_spec = pl.BlockSpec((PEER_HEADS, ntile, PEER_NKEYS, PEER_LT), lambda t, c: (0, t, 0, 0))
    vec = lambda: pl.BlockSpec((1, D_MODEL), lambda t, c: (0, 0))
    est = (4 * PEER_HEADS * PEER_NKEYS * PEER_TOK * 2 + 4 * PEER_HEADS * PEER_IB * PEER_TOK * 4
           + 4 * PEER_ECHUNK * D_MODEL * 2 + 2 * PEER_TOK * D_MODEL * (2 + 4 + 4)
           + PEER_ECHUNK * PEER_TOK * 6 + 2 * D_MODEL * PEER_TOK * 4)
    return pl.pallas_call(
        _peer_kernel,
        out_shape=jax.ShapeDtypeStruct((n, D_MODEL), f32),
        grid=(n // PEER_TOK, PEER_EXPERTS // PEER_ECHUNK),
        in_specs=[
            pl.BlockSpec((PEER_TOK, D_MODEL), lambda t, c: (t, 0)),
            pl.BlockSpec((PEER_ECHUNK, D_MODEL), lambda t, c: (c, 0)),
            pl.BlockSpec((PEER_ECHUNK, D_MODEL), lambda t, c: (c, 0)),
            row_spec, row_spec, tile_spec, tile_spec,
            pl.BlockSpec((PEER_TOK, D_MODEL), lambda t, c: (t, 0)),
            pl.BlockSpec((None, 6, D_MODEL), lambda t, c: (t // tiles_per_batch, 0, 0)),
            vec(), vec(),
        ],
        out_specs=pl.BlockSpec((PEER_TOK, D_MODEL), lambda t, c: (t, 0)),
        scratch_shapes=[
            pltpu.VMEM((D_MODEL, PEER_TOK), bf16),
            pltpu.VMEM((PEER_ECHUNK, PEER_TOK), f32),
            pltpu.VMEM((PEER_ECHUNK, PEER_TOK), bf16),
            pltpu.VMEM((PEER_TOK, D_MODEL), f32),
        ],
        compiler_params=_params(("parallel", "arbitrary"), est),
        name="peer_experts",
    )(h2, u, v, cnt, e1, rho, e2, y1, mod, g, b)


_QUARTER = HEAD_DIM // 4


def _swap_quarters(x):
    lead = x.shape[:-1]
    return jnp.flip(x.reshape(lead + (x.shape[-1] // (2 * _QUARTER), 2, _QUARTER)), axis=-2).reshape(x.shape)


def _rope_tables(t):
    pos = jnp.arange(t)
    rows = (pos // GRID_W).astype(f32)
    cols = (pos % GRID_W).astype(f32)
    inv = 1.0 / (ROPE_THETA ** (jnp.arange(_QUARTER, dtype=f32) / _QUARTER))
    ar, ac = rows[:, None] * inv, cols[:, None] * inv
    cos64 = jnp.concatenate([jnp.cos(ar), jnp.cos(ar), jnp.cos(ac), jnp.cos(ac)], axis=-1)
    sin64 = jnp.concatenate([-jnp.sin(ar), jnp.sin(ar), -jnp.sin(ac), jnp.sin(ac)], axis=-1)
    return jnp.tile(cos64, (1, 2)), jnp.tile(sin64, (1, 2))


def _head_mean_matrix():
    return jnp.asarray(np.kron(np.eye(LANES // HEAD_DIM), np.full((HEAD_DIM, HEAD_DIM), 1.0 / HEAD_DIM)), dtype=f32)


def _na_bias_table(rpb):
    w = np.arange(GRID_W)[:, None]
    c = np.arange(GRID_W)[None, :]
    cs = np.clip(w - NA_WIN_W // 2, 0, GRID_W - NA_WIN_W)
    inwin = (c >= cs) & (c < cs + NA_WIN_W)
    coff = np.clip(c - w + (NA_WIN_W - 1), 0, 2 * NA_WIN_W - 2)
    roff = np.arange(NA_WIN_H)[:, None] + np.arange(NA_WIN_H)[None, :]
    t = rpb[:, roff][:, :, :, coff]
    t = jnp.where(inwin[None, None, None], t, NEG_BIG)
    return t.transpose(0, 1, 3, 2, 4).reshape(NA_HEADS, NA_WIN_H, GRID_W, NA_WIN_H * GRID_W)


def _block_diag(w):
    g, c, _ = w.shape
    eye = jnp.eye(g, dtype=w.dtype)
    return (eye[:, None, :, None] * w[:, :, None, :]).reshape(g * c, g * c)


def _mod_rows(mod_l, start, count):
    return mod_l[start:start + count].reshape(count, 6, D_MODEL)


def kernel(x_prompt, x_sample, c, cache_na_k, cache_na_v, cache_gqa_k, cache_gqa_v, c_ctx, w_in, q_gain, k_gain,
           na_rpb, w_fourier, w_out, w_mod, b_mod, ln1_g, ln1_b, ln2_g, ln2_b, peer_wq, peer_subkeys, peer_u, peer_v):
    batch, seq, _ = x_prompt.shape
    dec_batch, dec_seq, _ = x_sample.shape
    past = cache_na_k.shape[2]
    n_ctx, n_lat = batch * seq, dec_batch * dec_seq

    cond = jnp.zeros((SUBLANES, D_MODEL), f32).at[0].set(c_ctx).at[1:1 + dec_batch].set(c)
    mod = _modulation(cond, w_mod, b_mod)

    bd = _head_mean_matrix()
    cos_t, sin_t = _rope_tables(dec_seq)

    yp = x_prompt.reshape(n_ctx, D_MODEL)
    ys = x_sample.reshape(n_lat, D_MODEL)
    na_k, na_v, gqa_k, gqa_v = [], [], [], []
    for l in range(DEPTH):
        w_in_l = w_in[l]
        w_ctx = w_in_l.astype(bf16)
        w_lat = jnp.concatenate([w_in_l, _swap_quarters(w_in_l[:, OFF_QB:OFF_KB]),
                                 _swap_quarters(w_in_l[:, OFF_KB:OFF_VB])], axis=1).astype(bf16)
        gq = jnp.tile(q_gain[l], GQA_Q_HEADS)[None, :]
        gk = jnp.tile(k_gain[l], GQA_KV_HEADS)[None, :]
        gqs = jnp.tile(_swap_quarters(q_gain[l]), GQA_Q_HEADS)[None, :]
        gks = jnp.tile(_swap_quarters(k_gain[l]), GQA_KV_HEADS)[None, :]
        wf_bd = _block_diag(w_fourier[l]).astype(bf16)
        w_out_l = w_out[l].astype(bf16)
        wq_l = peer_wq[l].astype(bf16)
        keys_l = peer_subkeys[l].reshape(2 * PEER_HEADS, PEER_NKEYS, PEER_QDIM // 2)
        u_l = peer_u[l].astype(bf16)
        v_l = peer_v[l].astype(bf16)
        bias_l = _na_bias_table(na_rpb[l])
        g1, b1 = ln1_g[l][None, :], ln1_b[l][None, :]
        g2, b2 = ln2_g[l][None, :], ln2_b[l][None, :]

        def ffn(y1, h2, mod_g, tiles_per_batch):
            cnt, e1, rho, e2 = _peer_query(h2, wq_l, keys_l)
            return _peer(h2, u_l, v_l, cnt, e1, rho, e2, y1, mod_g, g2, b2, tiles_per_batch)

        mod_c = _mod_rows(mod[l], 0, 1)
        tiles_c = n_ctx // ROW_TILE
        qa, ka, va, qb, kb, vb, xc = _inproj(yp, mod_c, w_ctx, bd, gq, gk, None, tiles_c)
        oa, ob = _ctx_attention(qa, ka, va, qb, kb, vb, seq)
        oc = _fourier(xc, wf_bd, batch, seq)
        y1, h2 = _outproj(oa, ob, oc, w_out_l, yp, mod_c, g1, b1, tiles_c)
        yp = ffn(y1, h2, mod_c, n_ctx // PEER_TOK)
        na_k.append(ka.reshape(batch, seq, NA_HEADS, HEAD_DIM))
        na_v.append(va.reshape(batch, seq, NA_HEADS, HEAD_DIM))
        gqa_k.append(kb.reshape(batch, seq, GQA_KV_HEADS, HEAD_DIM))
        gqa_v.append(vb.reshape(batch, seq, GQA_KV_HEADS, HEAD_DIM))

        mod_s = _mod_rows(mod[l], 1, dec_batch)
        tiles_s = dec_seq // ROW_TILE
        qa, ka, va, qb, kb, vb, xc = _inproj(ys, mod_s, w_lat, bd, gq, gk, (gqs, gks, cos_t, sin_t), tiles_s)
        oa = _lat_na_attention(qa, ka, va, cache_na_k[:, l].reshape(dec_batch, past, NA_WIDTH),
                               cache_na_v[:, l].reshape(dec_batch, past, NA_WIDTH), bias_l, dec_batch, dec_seq)
        ob = _lat_gqa_attention(qb, kb, vb, cache_gqa_k[:, l].reshape(dec_batch, past, KV_WIDTH),
                                cache_gqa_v[:, l].reshape(dec_batch, past, KV_WIDTH), dec_batch, dec_seq)
        oc = _fourier(xc, wf_bd, dec_batch, dec_seq)
        y1, h2 = _outproj(oa, ob, oc, w_out_l, ys, mod_s, g1, b1, tiles_s)
        ys = ffn(y1, h2, mod_s, dec_seq // PEER_TOK)

    return (yp.reshape(batch, seq, D_MODEL), ys.reshape(dec_batch, dec_seq, D_MODEL),
            jnp.stack(na_k, axis=1), jnp.stack(na_v, axis=1), jnp.stack(gqa_k, axis=1), jnp.stack(gqa_v, axis=1))
```
